```python
import math
import jax, jax.numpy as jnp
from jax import lax
import numpy as np

D_MODEL = 4096
BATCH = 4
SEQ = 2048
DEPTH = 1
DEC_BATCH = 128
DEC_SEQ = 1
PAST_LEN = 16384
PAGE_SIZE = 128

MIX_WIDTH = 2 * D_MODEL
SSD_WIDTH = MIX_WIDTH // 2
GDN_WIDTH = MIX_WIDTH - SSD_WIDTH
SSD_HEAD_DIM = 64
SSD_HEADS = SSD_WIDTH // SSD_HEAD_DIM
SSD_GROUPS = 8
SSD_HPG = SSD_HEADS // SSD_GROUPS
SSD_STATE = 128
SSD_CONV_DIM = SSD_WIDTH + 2 * SSD_GROUPS * SSD_STATE
GDN_HEAD_DIM = 128
GDN_HEADS = GDN_WIDTH // GDN_HEAD_DIM
GDN_QK_WIDTH = GDN_HEADS * GDN_HEAD_DIM
GDN_CONV_DIM = 2 * GDN_QK_WIDTH + GDN_WIDTH
CONV_K = 4
CHUNK = 64
D_IN_PROJ = SSD_WIDTH + SSD_CONV_DIM + SSD_HEADS + GDN_CONV_DIM + GDN_WIDTH + 2 * GDN_HEADS
ALPHA = (2 * DEPTH) ** 0.25
DEEPNORM_BETA = (8 * DEPTH) ** -0.25
NORM_EPS = 1e-6

kernel_name = 'hymba_ssd_gdn_deepnorm_step'


def _split_points():
    sizes = [SSD_WIDTH, SSD_CONV_DIM, SSD_HEADS, GDN_CONV_DIM, GDN_WIDTH, GDN_HEADS, GDN_HEADS]
    pts, acc = [], 0
    for s in sizes[:-1]:
        acc += s
        pts.append(acc)
    return pts


def _causal_conv_silu(u, prev, w, b):
    L = u.shape[1]
    full = jnp.concatenate([prev.astype(u.dtype), u], axis=1)
    acc = full[:, 0:L] * w[0]
    for j in range(1, CONV_K):
        acc = acc + full[:, j:j + L] * w[j]
    if b is not None:
        acc = acc + b
    return jax.nn.silu(acc), full[:, L:]


def _to_chunks(a, q, nc):
    pad = nc * q - a.shape[1]
    a = jnp.pad(a, [(0, 0), (0, pad)] + [(0, 0)] * (a.ndim - 2))
    a = a.reshape((a.shape[0], nc, q) + a.shape[2:])
    return jnp.moveaxis(a, 1, 0)


def _from_chunks(ys, L):
    ys = jnp.moveaxis(ys, 0, 1)
    return ys.reshape((ys.shape[0], -1) + ys.shape[3:])[:, :L]


def _ssd_scan(xdt, dA, Bm, Cm, h0):
    L = xdt.shape[1]
    q = min(CHUNK, L)
    nc = -(-L // q)
    tri = jnp.tril(jnp.ones((q, q), bool))[None, :, :, None, None]

    def step(h, inp):
        xdt_c, dA_c, B_c, C_c = inp
        cum = jnp.cumsum(dA_c, axis=1)
        decay = jnp.exp(jnp.where(tri, cum[:, :, None] - cum[:, None, :], -jnp.inf))
        cb = jnp.einsum('btgn,bsgn->btsg', C_c, B_c)
        y = jnp.einsum('btsg,btsgh,bsghp->btghp', cb, decay, xdt_c)
        y = y + jnp.einsum('btgn,bghpn->btghp', C_c, h) * jnp.exp(cum)[..., None]
        last = cum[:, -1]
        w_s = jnp.exp(last[:, None] - cum)
        h = h * jnp.exp(last)[..., None, None] + jnp.einsum('bsgn,bsgh,bsghp->bghpn', B_c, w_s, xdt_c)
        return h, y

    hT, ys = lax.scan(step, h0, (_to_chunks(xdt, q, nc), _to_chunks(dA, q, nc),
                                 _to_chunks(Bm, q, nc), _to_chunks(Cm, q, nc)))
    return _from_chunks(ys, L), hT


def _gdn_scan(qh, kh, vh, g, beta, S0):
    L = qh.shape[1]
    q = min(CHUNK, L)
    nc = -(-L // q)
    tri = jnp.tril(jnp.ones((q, q), bool))
    tri_strict = jnp.tril(jnp.ones((q, q), bool), -1)
    eye = jnp.eye(q, dtype=jnp.float32)

    def step(S, inp):
        q_c, k_c, v_c, g_c, b_c = inp
        cum = jnp.cumsum(g_c, axis=1)
        cum_h = jnp.moveaxis(cum, 1, 2)
        b_h = jnp.moveaxis(b_c, 1, 2)
        decay = jnp.exp(jnp.where(tri, cum_h[..., :, None] - cum_h[..., None, :], -jnp.inf))
        kk = jnp.einsum('bthd,bshd->bhts', k_c, k_c)
        M = jnp.where(tri_strict, kk * decay * b_h[..., :, None], 0.0) + eye
        v_beta = jnp.einsum('bthd,bth->bhtd', v_c, b_c)
        k_beta = jnp.einsum('bthd,bth->bhtd', k_c, b_c * jnp.exp(cum))
        U = lax.linalg.triangular_solve(M, v_beta, left_side=True, lower=True)
        W = lax.linalg.triangular_solve(M, k_beta, left_side=True, lower=True)
        v_new = U - jnp.einsum('bhtk,bhkv->bhtv', W, S)
        qk = jnp.einsum('bthd,bshd->bhts', q_c, k_c) * decay
        o = (jnp.einsum('bthk,bhkv->bthv', q_c * jnp.exp(cum)[..., None], S)
             + jnp.einsum('bhts,bhsv->bthv', qk, v_new))
        last = cum_h[..., -1]
        S = (S * jnp.exp(last)[..., None, None]
             + jnp.einsum('bshk,bhs,bhsv->bhkv', k_c, jnp.exp(last[..., None] - cum_h), v_new))
        return S, o

    ST, os_ = lax.scan(step, S0, (_to_chunks(qh, q, nc), _to_chunks(kh, q, nc), _to_chunks(vh, q, nc),
                                  _to_chunks(g, q, nc), _to_chunks(beta, q, nc)))
    return _from_chunks(os_, L), ST


def _rms(x):
    return x * lax.rsqrt(jnp.mean(x * x, axis=-1, keepdims=True) + NORM_EPS)


def _l2norm(x):
    return x * lax.rsqrt(jnp.sum(x * x, axis=-1, keepdims=True) + NORM_EPS)


def _layernorm(x, g, b):
    xf = x.astype(jnp.float32)
    mu = jnp.mean(xf, axis=-1, keepdims=True)
    xc = xf - mu
    var = jnp.mean(xc * xc, axis=-1, keepdims=True)
    return (xc * lax.rsqrt(var + 1e-5) * g + b).astype(x.dtype)


def _layer(x, h_ssd, conv_ssd, s_gdn, conv_gdn, w_in, ssd_conv_w, ssd_conv_b, ssd_dt_bias, ssd_a_log,
           ssd_d, ssd_norm_w, gdn_conv_w, gdn_dt_bias, gdn_a_log, gdn_norm_w, w_out, ln_g, ln_b):
    f32 = jnp.float32
    Bsz, L, _ = x.shape
    proj = x @ w_in
    z_s, xbc, dt_raw, qkv, z_g, b_raw, a_raw = jnp.split(proj, _split_points(), axis=-1)

    xbc, conv_ssd_new = _causal_conv_silu(xbc, conv_ssd, ssd_conv_w, ssd_conv_b)
    xs, Bm, Cm = jnp.split(xbc.astype(f32), [SSD_WIDTH, SSD_WIDTH + SSD_GROUPS * SSD_STATE], axis=-1)
    xs = xs.reshape(Bsz, L, SSD_GROUPS, SSD_HPG, SSD_HEAD_DIM)
    Bm = Bm.reshape(Bsz, L, SSD_GROUPS, SSD_STATE)
    Cm = Cm.reshape(Bsz, L, SSD_GROUPS, SSD_STATE)
    dt = jax.nn.softplus(dt_raw.astype(f32) + ssd_dt_bias).reshape(Bsz, L, SSD_GROUPS, SSD_HPG)
    A = -jnp.exp(ssd_a_log.astype(f32)).reshape(SSD_GROUPS, SSD_HPG)
    h0 = h_ssd.astype(f32).reshape(Bsz, SSD_GROUPS, SSD_HPG, SSD_HEAD_DIM, SSD_STATE)
    y, h_new = _ssd_scan(xs * dt[..., None], dt * A, Bm, Cm, h0)
    y = y + xs * ssd_d.astype(f32).reshape(SSD_GROUPS, SSD_HPG)[..., None]
    u = y.reshape(Bsz, L, SSD_WIDTH) * jax.nn.silu(z_s.astype(f32))
    y_ssd = _rms(u.reshape(Bsz, L, SSD_GROUPS, -1)).reshape(Bsz, L, SSD_WIDTH) * ssd_norm_w

    qkv, conv_gdn_new = _causal_conv_silu(qkv, conv_gdn, gdn_conv_w, None)
    qg, kg, vg = jnp.split(qkv.astype(f32), [GDN_QK_WIDTH, 2 * GDN_QK_WIDTH], axis=-1)
    qg = _l2norm(qg.reshape(Bsz, L, GDN_HEADS, GDN_HEAD_DIM)) * (GDN_HEAD_DIM ** -0.5)
    kg = _l2norm(kg.reshape(Bsz, L, GDN_HEADS, GDN_HEAD_DIM))
    vg = vg.reshape(Bsz, L, GDN_HEADS, GDN_HEAD_DIM)
    beta = jax.nn.sigmoid(b_raw.astype(f32))
    g = -jnp.exp(gdn_a_log.astype(f32)) * jax.nn.softplus(a_raw.astype(f32) + gdn_dt_bias)
    o, s_new = _gdn_scan(qg, kg, vg, g, beta, s_gdn.astype(f32))
    zg = jax.nn.silu(z_g.astype(f32)).reshape(Bsz, L, GDN_HEADS, GDN_HEAD_DIM)
    y_gdn = (_rms(o) * gdn_norm_w * zg).reshape(Bsz, L, GDN_WIDTH)

    mix = jnp.concatenate([y_ssd, y_gdn], axis=-1).astype(x.dtype)
    out = mix @ w_out
    x_new = _layernorm(ALPHA * x + out, ln_g, ln_b)
    h_new = h_new.reshape(Bsz, SSD_HEADS, SSD_HEAD_DIM, SSD_STATE)
    return x_new, h_new, conv_ssd_new, s_new, conv_gdn_new


def setup_inputs(seed: int = 0) -> dict:
    key = jax.random.key(seed)
    ks = jax.random.split(key, 24)
    f32 = jnp.float32

    def nrm(k, shape, s):
        return s * jax.random.normal(k, shape, f32)

    def dt_bias(k, n):
        dt = jnp.exp(jax.random.uniform(k, (DEPTH, n), f32, math.log(1e-3), math.log(1e-1)))
        return dt + jnp.log(-jnp.expm1(-dt))

    def a_log(k, n):
        return jnp.log(jax.random.uniform(k, (DEPTH, n), f32, 1.0, 16.0))

    return {
        'x_prompt': nrm(ks[0], (BATCH, SEQ, D_MODEL), 1.0),
        'x_sample': nrm(ks[1], (DEC_BATCH, DEC_SEQ, D_MODEL), 1.0),
        'state_ssd': nrm(ks[2], (DEPTH, DEC_BATCH, SSD_HEADS, SSD_HEAD_DIM, SSD_STATE), 0.5),
        'state_ssd_conv': nrm(ks[3], (DEPTH, DEC_BATCH, CONV_K - 1, SSD_CONV_DIM), 1.0),
        'state_gdn': nrm(ks[4], (DEPTH, DEC_BATCH, GDN_HEADS, GDN_HEAD_DIM, GDN_HEAD_DIM), GDN_HEAD_DIM ** -0.5),
        'state_gdn_conv': nrm(ks[5], (DEPTH, DEC_BATCH, CONV_K - 1, GDN_CONV_DIM), 1.0),
        'w_in': nrm(ks[6], (DEPTH, D_MODEL, D_IN_PROJ), D_MODEL ** -0.5),
        'ssd_conv_w': nrm(ks[7], (DEPTH, CONV_K, SSD_CONV_DIM), CONV_K ** -0.5),
        'ssd_conv_b': nrm(ks[8], (DEPTH, SSD_CONV_DIM), 0.02),
        'ssd_dt_bias': dt_bias(ks[9], SSD_HEADS),
        'ssd_a_log': a_log(ks[10], SSD_HEADS),
        'ssd_d': 1.0 + nrm(ks[11], (DEPTH, SSD_HEADS), 0.1),
        'ssd_norm_w': 1.0 + nrm(ks[12], (DEPTH, SSD_WIDTH), 0.1),
        'gdn_conv_w': nrm(ks[13], (DEPTH, CONV_K, GDN_CONV_DIM), CONV_K ** -0.5),
        'gdn_dt_bias': dt_bias(ks[14], GDN_HEADS),
        'gdn_a_log': a_log(ks[15], GDN_HEADS),
        'gdn_norm_w': 1.0 + nrm(ks[16], (DEPTH, GDN_HEAD_DIM), 0.1),
        'w_out': nrm(ks[17], (DEPTH, MIX_WIDTH, D_MODEL), DEEPNORM_BETA * MIX_WIDTH ** -0.5),
        'ln_g': 1.0 + nrm(ks[18], (DEPTH, D_MODEL), 0.1),
        'ln_b': nrm(ks[19], (DEPTH, D_MODEL), 0.02),
    }


def reference(x_prompt, x_sample, state_ssd, state_ssd_conv, state_gdn, state_gdn_conv, w_in, ssd_conv_w,
              ssd_conv_b, ssd_dt_bias, ssd_a_log, ssd_d, ssd_norm_w, gdn_conv_w, gdn_dt_bias, gdn_a_log,
              gdn_norm_w, w_out, ln_g, ln_b):
    bp = x_prompt.shape[0]
    hp, hs = x_prompt, x_sample
    p_ssd, p_ssd_conv, p_gdn, p_gdn_conv = [], [], [], []
    s_ssd, s_ssd_conv, s_gdn, s_gdn_conv = [], [], [], []
    for l in range(DEPTH):
        wl = (w_in[l], ssd_conv_w[l], ssd_conv_b[l], ssd_dt_bias[l], ssd_a_log[l], ssd_d[l], ssd_norm_w[l],
              gdn_conv_w[l], gdn_dt_bias[l], gdn_a_log[l], gdn_norm_w[l], w_out[l], ln_g[l], ln_b[l])
        hp, a1, a2, a3, a4 = _layer(
            hp,
            jnp.zeros((bp, SSD_HEADS, SSD_HEAD_DIM, SSD_STATE), jnp.float32),
            jnp.zeros((bp, CONV_K - 1, SSD_CONV_DIM), x_prompt.dtype),
            jnp.zeros((bp, GDN_HEADS, GDN_HEAD_DIM, GDN_HEAD_DIM), jnp.float32),
            jnp.zeros((bp, CONV_K - 1, GDN_CONV_DIM), x_prompt.dtype),
            *wl)
        p_ssd.append(a1); p_ssd_conv.append(a2); p_gdn.append(a3); p_gdn_conv.append(a4)
        hs, b1, b2, b3, b4 = _layer(hs, state_ssd[l], state_ssd_conv[l], state_gdn[l], state_gdn_conv[l], *wl)
        s_ssd.append(b1); s_ssd_conv.append(b2); s_gdn.append(b3); s_gdn_conv.append(b4)
    return (hp, hs,
            jnp.stack(p_ssd), jnp.stack(p_ssd_conv), jnp.stack(p_gdn), jnp.stack(p_gdn_conv),
            jnp.stack(s_ssd), jnp.stack(s_ssd_conv), jnp.stack(s_gdn), jnp.stack(s_gdn_conv))
```

```python
import functools

import jax
import jax.numpy as jnp
from jax import lax
from jax.experimental import pallas as pl
from jax.experimental.pallas import tpu as pltpu

F32 = jnp.float32
BF16 = jnp.bfloat16

LANES = 128
SUBLANES = 8
SSD_HEAD_DIM = 64
SSD_GROUPS = 8
SSD_STATE = 128
GDN_HEAD_DIM = 128
CONV_K = 4
NORM_EPS = 1e-6
LN_EPS = 1e-5
CHUNK = 128
NEG_BIG = -1e30
VMEM_LIMIT = 56 * 1024 * 1024


def _divisor_tile(n, target, mult):
    best = None
    t = mult
    while t <= min(n, target):
        if n % t == 0:
            best = t
        t += mult
    assert best is not None, (n, target, mult)
    return best


def _dot(a, b):
    return jnp.dot(a, b, preferred_element_type=F32)


def _dot_nt(a, b):
    return lax.dot_general(a, b, (((1,), (1,)), ((), ())), preferred_element_type=F32)


def _split3(x):
    x1 = x.astype(BF16)
    r1 = x - x1.astype(F32)
    x2 = r1.astype(BF16)
    r2 = r1 - x2.astype(F32)
    return x1, x2, r2.astype(BF16)


def _dot_sel_lhs(sel_bf, x):
    x1, x2, x3 = _split3(x)
    return _dot(sel_bf, x1) + _dot(sel_bf, x2) + _dot(sel_bf, x3)


def _dot_sel_rhs(x, sel_bf):
    x1, x2, x3 = _split3(x)
    return _dot(x1, sel_bf) + _dot(x2, sel_bf) + _dot(x3, sel_bf)


def _dot_hilo(a, b):
    ah = a.astype(BF16)
    al = (a - ah.astype(F32)).astype(BF16)
    bh = b.astype(BF16)
    bl = (b - bh.astype(F32)).astype(BF16)
    return _dot(ah, bh) + _dot(ah, bl) + _dot(al, bh)


def _sigmoid(x):
    return 1.0 / (1.0 + jnp.exp(-x))


def _silu(x):
    return x * _sigmoid(x)


def _softplus(x):
    return jnp.maximum(x, 0.0) + jnp.log1p(jnp.exp(-jnp.abs(x)))


def _tril01(t, dtype=BF16):
    r = lax.broadcasted_iota(jnp.int32, (t, t), 0)
    c = lax.broadcasted_iota(jnp.int32, (t, t), 1)
    return (r >= c).astype(dtype)


def _expander(rows, cols, per, row_offset=0):
    assert per & (per - 1) == 0
    r = lax.broadcasted_iota(jnp.int32, (rows, cols), 0)
    c = lax.broadcasted_iota(jnp.int32, (rows, cols), 1)
    return (lax.shift_right_logical(c, per.bit_length() - 1) + row_offset == r).astype(BF16)


def _causal_conv_silu(raw_ref, ext_ref, w_ref, bias, t):
    ext_ref[SUBLANES:SUBLANES + t, :] = raw_ref[...]
    acc = ext_ref[SUBLANES - 3:SUBLANES - 3 + t, :] * w_ref[0:1, :]
    if bias is not None:
        acc = acc + bias
    for j in range(1, CONV_K):
        acc = acc + ext_ref[SUBLANES - 3 + j:SUBLANES - 3 + j + t, :] * w_ref[j:j + 1, :]
    ext_ref[0:SUBLANES, :] = ext_ref[t:t + SUBLANES, :]
    return _silu(acc)


def _lane_roll(x, shift):
    amount = jnp.where(shift == 0, 0, LANES - shift)
    return pltpu.roll(x, amount, 1)


def _inproj_kernel(x_ref, w_ref, wm_ref, o_ref, om_ref):
    x = x_ref[...]
    o_ref[...] = _dot(x, w_ref[...])

    @pl.when(pl.program_id(1) == 0)
    def _():
        om_ref[...] = _dot(x, wm_ref[...])


def _inproj(x_bf, w_main, w_misc):
    m, k = x_bf.shape
    n = w_main.shape[1]
    nm = w_misc.shape[1]
    tm = _divisor_tile(m, 1040, 16)
    tn = _divisor_tile(n, 1024, LANES)
    return pl.pallas_call(
        _inproj_kernel,
        grid=(m // tm, n // tn),
        in_specs=[
            pl.BlockSpec((tm, k), lambda i, j: (i, 0)),
            pl.BlockSpec((k, tn), lambda i, j: (0, j)),
            pl.BlockSpec((k, nm), lambda i, j: (0, 0)),
        ],
        out_specs=[
            pl.BlockSpec((tm, tn), lambda i, j: (i, j)),
            pl.BlockSpec((tm, nm), lambda i, j: (i, 0)),
        ],
        out_shape=[jax.ShapeDtypeStruct((m, n), F32), jax.ShapeDtypeStruct((m, nm), F32)],
        compiler_params=pltpu.CompilerParams(
            dimension_semantics=("parallel", "arbitrary"), vmem_limit_bytes=VMEM_LIMIT),
        name="inproj",
    )(x_bf, w_main, w_misc)


def _ssd_kernel(z_ref, xs_ref, b_ref, c_ref, misc_ref,
                wx_ref, wb_ref, wc_ref, bx_ref, bb_ref, bc_ref,
                dtb_ref, alog_ref, dvec_ref, nw_ref,
                y_ref, hout_ref,
                ht_ref, extx_ref, extb_ref, extc_ref, *, t, hpg, nc):
    g = pl.program_id(1)
    c = pl.program_id(2)
    wg = hpg * SSD_HEAD_DIM

    @pl.when(c == 0)
    def _():
        ht_ref[...] = jnp.zeros_like(ht_ref)
        extx_ref[0:SUBLANES, :] = jnp.zeros((SUBLANES, wg), F32)
        extb_ref[0:SUBLANES, :] = jnp.zeros((SUBLANES, SSD_STATE), F32)
        extc_ref[0:SUBLANES, :] = jnp.zeros((SUBLANES, SSD_STATE), F32)

    xs = _causal_conv_silu(xs_ref, extx_ref, wx_ref, bx_ref[...], t)
    bm = _causal_conv_silu(b_ref, extb_ref, wb_ref, bb_ref[...], t)
    cm = _causal_conv_silu(c_ref, extc_ref, wc_ref, bc_ref[...], t)

    misc = misc_ref[...]
    dt_all = _softplus(misc + dtb_ref[...])
    da_all = dt_all * (-jnp.exp(alog_ref[...]))
    dt_r = _lane_roll(dt_all, g * hpg)
    da_r = _lane_roll(da_all, g * hpg)

    tril = _tril01(t)
    cum = _dot_sel_lhs(tril, da_r)
    cum_t = cum.T
    expand = _expander(LANES, wg, SSD_HEAD_DIM)
    cum_b = _dot_sel_rhs(cum, expand)
    dt_b = _dot_sel_rhs(dt_r, expand)
    last_b = cum_b[t - 1:t, :]
    xdt = xs * dt_b

    bm_bf = bm.astype(BF16)
    cm_bf = cm.astype(BF16)
    cb = _dot_nt(cm_bf, bm_bf)
    rows = lax.broadcasted_iota(jnp.int32, (t, t), 0)
    cols = lax.broadcasted_iota(jnp.int32, (t, t), 1)
    causal = rows >= cols
    lane = lax.broadcasted_iota(jnp.int32, (t, LANES), 1)

    ys = []
    for p in range(hpg // 2):
        wms = []
        for h in (2 * p, 2 * p + 1):
            col = jnp.broadcast_to(cum[:, h:h + 1], (t, t))
            row = jnp.broadcast_to(cum_t[h:h + 1, :], (t, t))
            dec = jnp.exp(jnp.where(causal, col - row, NEG_BIG))
            wms.append((cb * dec).astype(BF16))
        lhs = jnp.concatenate(wms, axis=1)
        xp = xdt[:, p * LANES:(p + 1) * LANES]
        rhs = jnp.concatenate([jnp.where(lane < SSD_HEAD_DIM, xp, 0.0),
                               jnp.where(lane >= SSD_HEAD_DIM, xp, 0.0)], axis=0).astype(BF16)
        ys.append(_dot(lhs, rhs))
    y = jnp.concatenate(ys, axis=1) if len(ys) > 1 else ys[0]

    ht = ht_ref[...]
    y = y + _dot(cm_bf, ht.astype(BF16)) * jnp.exp(cum_b)
    y = y + xs * dvec_ref[...]
    u = y * _silu(z_ref[...])
    ms = jnp.mean(u * u, axis=-1, keepdims=True)
    y_ref[...] = (u * lax.rsqrt(ms + NORM_EPS) * nw_ref[...]).astype(y_ref.dtype)

    w_s = jnp.exp(last_b - cum_b)
    ht_new = ht * jnp.exp(last_b) + _dot(bm.T.astype(BF16), (xdt * w_s).astype(BF16))
    ht_ref[...] = ht_new

    @pl.when(c == nc - 1)
    def _():
        hout_ref[0] = ht_new.T.reshape(hpg, SSD_HEAD_DIM, SSD_STATE)


def _ssd_prompt(proj, misc, conv_w, conv_b, dtb_pad, alog_pad, dvec, norm_w, *, bsz, seqlen, dz):
    t = CHUNK
    nc = seqlen // t
    hs = dz // SSD_HEAD_DIM
    hpg = hs // SSD_GROUPS
    wg = hpg * SSD_HEAD_DIM
    ng = SSD_GROUPS
    xs_blk = dz // wg
    b_blk = 2 * dz // SSD_STATE
    c_blk = b_blk + ng
    cwb = dz // SSD_STATE
    row = lambda b, g, c: b * nc + c
    kern = functools.partial(_ssd_kernel, t=t, hpg=hpg, nc=nc)
    return pl.pallas_call(
        kern,
        grid=(bsz, ng, nc),
        in_specs=[
            pl.BlockSpec((t, wg), lambda b, g, c: (row(b, g, c), g)),
            pl.BlockSpec((t, wg), lambda b, g, c: (row(b, g, c), xs_blk + g)),
            pl.BlockSpec((t, SSD_STATE), lambda b, g, c: (row(b, g, c), b_blk + g)),
            pl.BlockSpec((t, SSD_STATE), lambda b, g, c: (row(b, g, c), c_blk + g)),
            pl.BlockSpec((t, LANES), lambda b, g, c: (row(b, g, c), 0)),
            pl.BlockSpec((CONV_K, wg), lambda b, g, c: (0, g)),
            pl.BlockSpec((CONV_K, SSD_STATE), lambda b, g, c: (0, cwb + g)),
            pl.BlockSpec((CONV_K, SSD_STATE), lambda b, g, c: (0, cwb + ng + g)),
            pl.BlockSpec((1, wg), lambda b, g, c: (0, g)),
            pl.BlockSpec((1, SSD_STATE), lambda b, g, c: (0, cwb + g)),
            pl.BlockSpec((1, SSD_STATE), lambda b, g, c: (0, cwb + ng + g)),
            pl.BlockSpec((1, LANES), lambda b, g, c: (0, 0)),
            pl.BlockSpec((1, LANES), lambda b, g, c: (0, 0)),
            pl.BlockSpec((1, wg), lambda b, g, c: (0, g)),
            pl.BlockSpec((1, wg), lambda b, g, c: (0, g)),
        ],
        out_specs=[
            pl.BlockSpec((t, wg), lambda b, g, c: (row(b, g, c), g)),
            pl.BlockSpec((1, hpg, SSD_HEAD_DIM, SSD_STATE), lambda b, g, c: (b, g, 0, 0)),
        ],
        out_shape=[
            jax.ShapeDtypeStruct((bsz * seqlen, dz), BF16),
            jax.ShapeDtypeStruct((bsz, hs, SSD_HEAD_DIM, SSD_STATE), F32),
        ],
        scratch_shapes=[
            pltpu.VMEM((SSD_STATE, wg), F32),
            pltpu.VMEM((t + SUBLANES, wg), F32),
            pltpu.VMEM((t + SUBLANES, SSD_STATE), F32),
            pltpu.VMEM((t + SUBLANES, SSD_STATE), F32),
        ],
        compiler_params=pltpu.CompilerParams(
            dimension_semantics=("parallel", "parallel", "arbitrary"), vmem_limit_bytes=VMEM_LIMIT),
        name="ssd_prompt",
    )(proj, proj, proj, proj, misc, conv_w, conv_w, conv_w, conv_b, conv_b, conv_b,
      dtb_pad, alog_pad, dvec, norm_w)


def _unit_lower_inverse(n, t):
    q = -n
    nk = n
    k = 2
    while k < t:
        nk = _dot_hilo(nk, nk)
        q = q + nk + _dot_hilo(q, nk)
        k *= 2
    rows = lax.broadcasted_iota(jnp.int32, (t, t), 0)
    cols = lax.broadcasted_iota(jnp.int32, (t, t), 1)
    return jnp.where(rows == cols, 1.0, 0.0) + q


def _gdn_kernel(q_ref, k_ref, v_ref, zg_ref, misc_ref,
                wq_ref, wk_ref, wv_ref, dtb_ref, alog_ref, nw_ref,
                y_ref, sout_ref,
                s_ref, extq_ref, extk_ref, extv_ref, *, t, hb, nc, b_lane, a_lane):
    hblk = pl.program_id(1)
    c = pl.program_id(2)
    d = GDN_HEAD_DIM

    @pl.when(c == 0)
    def _():
        s_ref[...] = jnp.zeros_like(s_ref)
        for ext in (extq_ref, extk_ref, extv_ref):
            ext[0:SUBLANES, :] = jnp.zeros((SUBLANES, hb * d), F32)

    q = _causal_conv_silu(q_ref, extq_ref, wq_ref, None, t)
    k = _causal_conv_silu(k_ref, extk_ref, wk_ref, None, t)
    v = _causal_conv_silu(v_ref, extv_ref, wv_ref, None, t)
    zg = zg_ref[...]

    misc = misc_ref[...]
    beta_r = _lane_roll(_sigmoid(misc), b_lane + hblk * hb)
    g_all = -jnp.exp(alog_ref[...]) * _softplus(misc + dtb_ref[...])
    g_r = _lane_roll(g_all, a_lane + hblk * hb)

    tril = _tril01(t)
    cum = _dot_sel_lhs(tril, g_r)
    cum_t = cum.T
    ecum = jnp.exp(cum)
    last = cum[t - 1:t, :]
    wdec = jnp.exp(last - cum)
    elast = jnp.exp(last)

    rows = lax.broadcasted_iota(jnp.int32, (t, t), 0)
    cols = lax.broadcasted_iota(jnp.int32, (t, t), 1)
    causal = rows >= cols
    strict = rows > cols

    outs = []
    for i in range(hb):
        sl = slice(i * d, (i + 1) * d)
        qi, ki, vi = q[:, sl], k[:, sl], v[:, sl]
        qn = qi * lax.rsqrt(jnp.sum(qi * qi, axis=-1, keepdims=True) + NORM_EPS) * (d ** -0.5)
        kn = ki * lax.rsqrt(jnp.sum(ki * ki, axis=-1, keepdims=True) + NORM_EPS)
        col = jnp.broadcast_to(cum[:, i:i + 1], (t, t))
        row = jnp.broadcast_to(cum_t[i:i + 1, :], (t, t))
        dec = jnp.exp(jnp.where(causal, col - row, NEG_BIG))
        beta_c = jnp.broadcast_to(beta_r[:, i:i + 1], (t, d))
        ecum_c = jnp.broadcast_to(ecum[:, i:i + 1], (t, d))
        wdec_c = jnp.broadcast_to(wdec[:, i:i + 1], (t, d))
        kn_bf = kn.astype(BF16)
        qn_bf = qn.astype(BF16)
        kk = _dot_nt(kn_bf, kn_bf)
        n = jnp.where(strict, kk * dec * jnp.broadcast_to(beta_r[:, i:i + 1], (t, t)), 0.0)
        minv = _unit_lower_inverse(n, t)
        rhs = jnp.concatenate([vi * beta_c, kn * beta_c * ecum_c], axis=1).astype(BF16)
        x = _dot(minv.astype(BF16), rhs)
        u, w = x[:, :d], x[:, d:]
        s = s_ref[i]
        r = _dot(jnp.concatenate([w, qn * ecum_c], axis=0).astype(BF16), s.astype(BF16))
        v_new = u - r[:t]
        v_new_bf = v_new.astype(BF16)
        qk = _dot_nt(qn_bf, kn_bf) * dec
        o = r[t:] + _dot(qk.astype(BF16), v_new_bf)
        s_new = s * elast[:, i:i + 1] + _dot((kn * wdec_c).T.astype(BF16), v_new_bf)
        s_ref[i] = s_new
        ms = jnp.mean(o * o, axis=-1, keepdims=True)
        outs.append(o * lax.rsqrt(ms + NORM_EPS) * nw_ref[...] * _silu(zg[:, sl]))

        @pl.when(c == nc - 1)
        def _(s_new=s_new, i=i):
            sout_ref[0, i] = s_new

    y = jnp.concatenate(outs, axis=1) if hb > 1 else outs[0]
    y_ref[...] = y.astype(y_ref.dtype)


def _gdn_prompt(proj, misc, conv_w, dtb_pad, alog_pad, norm_w, *, bsz, seqlen, q_off, zg_off, width,
                b_lane, a_lane):
    t = CHUNK
    nc = seqlen // t
    d = GDN_HEAD_DIM
    hg = width // d
    hb = 2
    bw = hb * d
    nhb = hg // hb
    qb, kb, vb, zb = q_off // bw, (q_off + width) // bw, (q_off + 2 * width) // bw, zg_off // bw
    wkb, wvb = width // bw, 2 * width // bw
    row = lambda b, h, c: b * nc + c
    kern = functools.partial(_gdn_kernel, t=t, hb=hb, nc=nc, b_lane=b_lane, a_lane=a_lane)
    return pl.pallas_call(
        kern,
        grid=(bsz, nhb, nc),
        in_specs=[
            pl.BlockSpec((t, bw), lambda b, h, c: (row(b, h, c), qb + h)),
            pl.BlockSpec((t, bw), lambda b, h, c: (row(b, h, c), kb + h)),
            pl.BlockSpec((t, bw), lambda b, h, c: (row(b, h, c), vb + h)),
            pl.BlockSpec((t, bw), lambda b, h, c: (row(b, h, c), zb + h)),
            pl.BlockSpec((t, LANES), lambda b, h, c: (row(b, h, c), 0)),
            pl.BlockSpec((CONV_K, bw), lambda b, h, c: (0, h)),
            pl.BlockSpec((CONV_K, bw), lambda b, h, c: (0, wkb + h)),
            pl.BlockSpec((CONV_K, bw), lambda b, h, c: (0, wvb + h)),
            pl.BlockSpec((1, LANES), lambda b, h, c: (0, 0)),
            pl.BlockSpec((1, LANES), lambda b, h, c: (0, 0)),
            pl.BlockSpec((1, d), lambda b, h, c: (0, 0)),
        ],
        out_specs=[
            pl.BlockSpec((t, bw), lambda b, h, c: (row(b, h, c), h)),
            pl.BlockSpec((1, hb, d, d), lambda b, h, c: (b, h, 0, 0)),
        ],
        out_shape=[
            jax.ShapeDtypeStruct((bsz * seqlen, width), BF16),
            jax.ShapeDtypeStruct((bsz, hg, d, d), F32),
        ],
        scratch_shapes=[
            pltpu.VMEM((hb, d, d), F32),
            pltpu.VMEM((t + SUBLANES, bw), F32),
            pltpu.VMEM((t + SUBLANES, bw), F32),
            pltpu.VMEM((t + SUBLANES, bw), F32),
        ],
        compiler_params=pltpu.CompilerParams(
            dimension_semantics=("parallel", "parallel", "arbitrary"), vmem_limit_bytes=VMEM_LIMIT),
        name="gdn_prompt",
    )(proj, proj, proj, proj, misc, conv_w, conv_w, conv_w, dtb_pad, alog_pad, norm_w)


def _sample_conv_kernel(s0_ref, s1_ref, s2_ref, u_ref, w_ref, b_ref, o_ref):
    acc = b_ref[...] + s0_ref[...] * w_ref[0:1, :]
    acc = acc + s1_ref[...] * w_ref[1:2, :]
    acc = acc + s2_ref[...] * w_ref[2:3, :]
    acc = acc + u_ref[...] * w_ref[3:4, :]
    o_ref[...] = _silu(acc)


def _sample_conv(state2d, proj, conv_w, conv_b, *, ms, row_blk, col_off, width):
    wb = _divisor_tile(width, 512, LANES)
    nb = width // wb
    cb = col_off // wb
    return pl.pallas_call(
        _sample_conv_kernel,
        grid=(nb,),
        in_specs=[
            pl.BlockSpec((ms, wb), lambda j: (0, j)),
            pl.BlockSpec((ms, wb), lambda j: (0, nb + j)),
            pl.BlockSpec((ms, wb), lambda j: (0, 2 * nb + j)),
            pl.BlockSpec((ms, wb), lambda j: (row_blk, cb + j)),
            pl.BlockSpec((CONV_K, wb), lambda j: (0, j)),
            pl.BlockSpec((1, wb), lambda j: (0, j)),
        ],
        out_specs=pl.BlockSpec((ms, wb), lambda j: (0, j)),
        out_shape=jax.ShapeDtypeStruct((ms, width), F32),
        compiler_params=pltpu.CompilerParams(dimension_semantics=("parallel",)),
        name="sample_conv",
    )(state2d, state2d, state2d, proj, conv_w, conv_b)


def _sample_scalars_kernel(misc_ref, dtb_ref, alog_s_ref, dtb_g_ref, alog_g_ref, dt_ref, sc_ref, *, hs, hg):
    misc = misc_ref[...]
    lane = lax.broadcasted_iota(jnp.int32, misc.shape, 1)
    dt = _softplus(misc + dtb_ref[...])
    eda = jnp.exp(dt * (-jnp.exp(alog_s_ref[...])))
    beta = _sigmoid(misc)
    eg = jnp.exp(-jnp.exp(alog_g_ref[...]) * _softplus(misc + dtb_g_ref[...]))
    dt_ref[...] = dt
    sc_ref[...] = jnp.where(lane < hs, eda, jnp.where(lane < hs + hg, beta, eg))


def _sample_scalars(misc, dtb_s, alog_s, dtb_g, alog_g, *, ms, row_blk, hs, hg):
    vec = pl.BlockSpec((1, LANES), lambda i: (0, 0))
    return pl.pallas_call(
        functools.partial(_sample_scalars_kernel, hs=hs, hg=hg),
        grid=(1,),
        in_specs=[pl.BlockSpec((ms, LANES), lambda i: (row_blk, 0)), vec, vec, vec, vec],
        out_specs=[pl.BlockSpec((ms, LANES), lambda i: (0, 0)), pl.BlockSpec((ms, LANES), lambda i: (0, 0))],
        out_shape=[jax.ShapeDtypeStruct((ms, LANES), F32), jax.ShapeDtypeStruct((ms, LANES), F32)],
        name="sample_scalars",
    )(misc, dtb_s, alog_s, dtb_g, alog_g)


def _transpose_blocks(x, nblk):
    return jnp.concatenate([x[:, j * LANES:(j + 1) * LANES].T for j in range(nblk)], axis=0)


def _row_group(b):
    b8 = pl.multiple_of(lax.shift_right_logical(b, 3) * SUBLANES, SUBLANES)
    in_group = lax.broadcasted_iota(jnp.int32, (SUBLANES, LANES), 0) == jnp.bitwise_and(b, SUBLANES - 1)
    return b8, in_group


def _untranspose_blocks(x, nblk):
    return jnp.concatenate([x[j * LANES:(j + 1) * LANES, :].T for j in range(nblk)], axis=1)


def _ssd_sample_kernel(sc_smem, h_ref, xbc_ref, dt_ref, z_ref, dvec_ref, nw_ref,
                       y_ref, hout_ref,
                       xt_ref, yt_ref, *, ms, hs, dz):
    b = pl.program_id(0)
    hpg = hs // SSD_GROUPS
    wg = hpg * SSD_HEAD_DIM

    @pl.when(b == 0)
    def _():
        xs = xbc_ref[:, 0:dz]
        dt_b = _dot_sel_rhs(dt_ref[...], _expander(LANES, dz, SSD_HEAD_DIM))
        xt_ref[...] = _transpose_blocks(xs * dt_b, dz // LANES)
        yt_ref[...] = jnp.zeros_like(yt_ref)

    seq_rows = lax.broadcasted_iota(jnp.int32, (ms, SSD_STATE), 0)
    seq_lanes = lax.broadcasted_iota(jnp.int32, (wg, LANES), 1)
    b8, in_group = _row_group(b)
    for g in range(SSD_GROUPS):
        hgrp = h_ref[0, g * hpg:(g + 1) * hpg].reshape(wg, SSD_STATE)
        bm = xbc_ref[:, dz + g * SSD_STATE: dz + (g + 1) * SSD_STATE]
        bm = jnp.where(seq_rows == b, bm, 0.0).astype(BF16)
        upd = _dot(xt_ref[g * wg:(g + 1) * wg, :].astype(BF16), bm)
        decay = jnp.concatenate(
            [jnp.full((SSD_HEAD_DIM, SSD_STATE), sc_smem[b, g * hpg + h], F32) for h in range(hpg)], axis=0)
        hn = hgrp * decay + upd
        hout_ref[0, g * hpg:(g + 1) * hpg] = hn.reshape(hpg, SSD_HEAD_DIM, SSD_STATE)
        c_off = dz + SSD_GROUPS * SSD_STATE + g * SSD_STATE
        crow = jnp.sum(jnp.where(in_group, xbc_ref[pl.ds(b8, SUBLANES), c_off:c_off + SSD_STATE], 0.0),
                       axis=0, keepdims=True)
        ycol = jnp.sum(hn * crow, axis=1, keepdims=True)
        cur = yt_ref[g * wg:(g + 1) * wg, :]
        yt_ref[g * wg:(g + 1) * wg, :] = jnp.where(seq_lanes == b, ycol, cur)

    @pl.when(b == ms - 1)
    def _():
        y = _untranspose_blocks(yt_ref[...], dz // LANES)
        y = y + xbc_ref[:, 0:dz] * dvec_ref[...]
        u = y * _silu(z_ref[...])
        outs = []
        for g in range(SSD_GROUPS):
            ug = u[:, g * wg:(g + 1) * wg]
            msq = jnp.mean(ug * ug, axis=-1, keepdims=True)
            outs.append(ug * lax.rsqrt(msq + NORM_EPS))
        y_ref[...] = (jnp.concatenate(outs, axis=1) * nw_ref[...]).astype(y_ref.dtype)


def _ssd_sample(scal, h_state, xbc_s, dt_s, proj, dvec, norm_w, *, ms, row_blk, dz):
    hs = dz // SSD_HEAD_DIM
    cw = xbc_s.shape[1]
    full = lambda shape: pl.BlockSpec(shape, lambda b: tuple(0 for _ in shape))
    return pl.pallas_call(
        functools.partial(_ssd_sample_kernel, ms=ms, hs=hs, dz=dz),
        grid=(ms,),
        in_specs=[
            pl.BlockSpec(memory_space=pltpu.SMEM),
            pl.BlockSpec((1, hs, SSD_HEAD_DIM, SSD_STATE), lambda b: (b, 0, 0, 0)),
            full((ms, cw)),
            full((ms, LANES)),
            pl.BlockSpec((ms, dz), lambda b: (row_blk, 0)),
            full((1, dz)),
            full((1, dz)),
        ],
        out_specs=[
            full((ms, dz)),
            pl.BlockSpec((1, hs, SSD_HEAD_DIM, SSD_STATE), lambda b: (b, 0, 0, 0)),
        ],
        out_shape=[
            jax.ShapeDtypeStruct((ms, dz), BF16),
            jax.ShapeDtypeStruct(h_state.shape, F32),
        ],
        scratch_shapes=[pltpu.VMEM((dz, LANES), F32), pltpu.VMEM((dz, LANES), F32)],
        compiler_params=pltpu.CompilerParams(
            dimension_semantics=("arbitrary",), vmem_limit_bytes=VMEM_LIMIT),
        name="ssd_sample",
    )(scal, h_state, xbc_s, dt_s, proj, dvec, norm_w)


def _gdn_sample_kernel(sc_smem, s_ref, qkv_ref, sc_ref, zg0_ref, zg1_ref, nw_ref,
                       y_ref, sout_ref,
                       w_ref, qe_ref, vb_ref, qk_ref, kt_ref, o_ref, *, ms, hs, hg):
    b = pl.program_id(0)
    d = GDN_HEAD_DIM
    width = hg * d

    @pl.when(b == 0)
    def _():
        sc = sc_ref[...]
        beta_b = _dot_sel_rhs(sc, _expander(LANES, width, d, hs))
        eg_b = _dot_sel_rhs(sc, _expander(LANES, width, d, hs + hg))
        for h in range(hg):
            sl = slice(h * d, (h + 1) * d)
            qh = qkv_ref[:, h * d:(h + 1) * d]
            kh = qkv_ref[:, width + h * d: width + (h + 1) * d]
            vh = qkv_ref[:, 2 * width + h * d: 2 * width + (h + 1) * d]
            qn = qh * lax.rsqrt(jnp.sum(qh * qh, axis=-1, keepdims=True) + NORM_EPS) * (d ** -0.5)
            kn = kh * lax.rsqrt(jnp.sum(kh * kh, axis=-1, keepdims=True) + NORM_EPS)
            w_ref[:, sl] = kn * beta_b[:, sl] * eg_b[:, sl]
            qe_ref[:, sl] = qn * eg_b[:, sl]
            vb_ref[:, sl] = vh * beta_b[:, sl]
            qk_ref[:, sl] = jnp.broadcast_to(jnp.sum(qn * kn, axis=-1, keepdims=True), (ms, d))
            kt_ref[sl, :] = kn.T

    seq_rows = lax.broadcasted_iota(jnp.int32, (ms, d), 0)
    b8, in_group = _row_group(b)
    for h in range(hg):
        sl = slice(h * d, (h + 1) * d)
        s = s_ref[0, h]
        w8 = jnp.where(in_group, w_ref[pl.ds(b8, SUBLANES), sl], 0.0)
        q8 = jnp.where(in_group, qe_ref[pl.ds(b8, SUBLANES), sl], 0.0)
        r = _dot(jnp.concatenate([w8, q8], axis=0).astype(BF16), s.astype(BF16))
        vb8 = jnp.where(in_group, vb_ref[pl.ds(b8, SUBLANES), sl], 0.0)
        v_new8 = vb8 - r[0:SUBLANES]
        v_new = jnp.sum(v_new8, axis=0, keepdims=True)
        o8 = r[SUBLANES:] + qk_ref[pl.ds(b8, SUBLANES), sl] * v_new8
        o_ref[pl.ds(b8, SUBLANES), sl] = jnp.where(in_group, o8, o_ref[pl.ds(b8, SUBLANES), sl])
        vm = jnp.where(seq_rows == b, jnp.broadcast_to(v_new, (ms, d)), 0.0).astype(BF16)
        s_new = s * sc_smem[b, hs + hg + h] + _dot(kt_ref[sl, :].astype(BF16), vm)
        sout_ref[0, h] = s_new

    @pl.when(b == ms - 1)
    def _():
        outs = []
        for h in range(hg):
            sl = slice(h * d, (h + 1) * d)
            o = o_ref[:, sl]
            msq = jnp.mean(o * o, axis=-1, keepdims=True)
            zg_ref, zoff = (zg0_ref, 0) if h < hg // 2 else (zg1_ref, width // 2)
            zg = zg_ref[:, h * d - zoff:(h + 1) * d - zoff]
            outs.append(o * lax.rsqrt(msq + NORM_EPS) * nw_ref[...] * _silu(zg))
        y_ref[...] = jnp.concatenate(outs, axis=1).astype(y_ref.dtype)


def _gdn_sample(scal, s_state, qkv_s, proj, norm_w, *, ms, row_blk, hs, zg_off):
    d = GDN_HEAD_DIM
    hg = s_state.shape[1]
    width = hg * d
    full = lambda shape: pl.BlockSpec(shape, lambda b: tuple(0 for _ in shape))
    act = pltpu.VMEM((ms, width), F32)
    return pl.pallas_call(
        functools.partial(_gdn_sample_kernel, ms=ms, hs=hs, hg=hg),
        grid=(ms,),
        in_specs=[
            pl.BlockSpec(memory_space=pltpu.SMEM),
            pl.BlockSpec((1, hg, d, d), lambda b: (b, 0, 0, 0)),
            full((ms, 3 * width)),
            full((ms, LANES)),
            pl.BlockSpec((ms, width // 2), lambda b: (row_blk, zg_off // (width // 2))),
            pl.BlockSpec((ms, width // 2), lambda b: (row_blk, zg_off // (width // 2) + 1)),
            full((1, d)),
        ],
        out_specs=[
            full((ms, width)),
            pl.BlockSpec((1, hg, d, d), lambda b: (b, 0, 0, 0)),
        ],
        out_shape=[
            jax.ShapeDtypeStruct((ms, width), BF16),
            jax.ShapeDtypeStruct(s_state.shape, F32),
        ],
        scratch_shapes=[act, act, act, act, pltpu.VMEM((width, LANES), F32), act],
        compiler_params=pltpu.CompilerParams(
            dimension_semantics=("arbitrary",), vmem_limit_bytes=VMEM_LIMIT),
        name="gdn_sample",
    )(scal, s_state, qkv_s, scal, proj, proj, norm_w)


def _outproj_kernel(ya_ref, yb_ref, w_ref, x_ref, g_ref, b_ref, o_ref, *, nkh, nk, alpha):
    k = pl.program_id(1)

    @pl.when(k == 0)
    def _():
        o_ref[...] = jnp.zeros_like(o_ref)

    @pl.when(k < nkh)
    def _():
        o_ref[...] += _dot(ya_ref[...], w_ref[...])

    @pl.when(k >= nkh)
    def _():
        o_ref[...] += _dot(yb_ref[...], w_ref[...])

    @pl.when(k == nk - 1)
    def _():
        y = alpha * x_ref[...] + o_ref[...]
        mu = jnp.mean(y, axis=-1, keepdims=True)
        yc = y - mu
        var = jnp.mean(yc * yc, axis=-1, keepdims=True)
        o_ref[...] = yc * lax.rsqrt(var + LN_EPS) * g_ref[...] + b_ref[...]


def _outproj(ya, yb, w_bf, x, ln_g, ln_b, *, alpha):
    m, ka = ya.shape
    kb = yb.shape[1]
    d = w_bf.shape[1]
    tm = _divisor_tile(m, 256, SUBLANES)
    tk = _divisor_tile(min(ka, kb), 1024, LANES)
    assert ka % tk == 0 and kb % tk == 0
    nkh = ka // tk
    nk = nkh + kb // tk
    return pl.pallas_call(
        functools.partial(_outproj_kernel, nkh=nkh, nk=nk, alpha=alpha),
        grid=(m // tm, nk),
        in_specs=[
            pl.BlockSpec((tm, tk), lambda i, k: (i, jnp.minimum(k, nkh - 1))),
            pl.BlockSpec((tm, tk), lambda i, k: (i, jnp.maximum(k - nkh, 0))),
            pl.BlockSpec((tk, d), lambda i, k: (k, 0)),
            pl.BlockSpec((tm, d), lambda i, k: (i, 0)),
            pl.BlockSpec((1, d), lambda i, k: (0, 0)),
            pl.BlockSpec((1, d), lambda i, k: (0, 0)),
        ],
        out_specs=pl.BlockSpec((tm, d), lambda i, k: (i, 0)),
        out_shape=jax.ShapeDtypeStruct((m, d), F32),
        compiler_params=pltpu.CompilerParams(
            dimension_semantics=("parallel", "arbitrary"), vmem_limit_bytes=VMEM_LIMIT),
        name="outproj",
    )(ya, yb, w_bf, x, ln_g, ln_b)


def _pad_lanes(vec, offset):
    out = jnp.zeros((1, LANES), F32)
    return lax.dynamic_update_slice(out, vec.astype(F32)[None, :], (0, offset))


def _layer(xp, xs_in, st_ssd, st_ssd_conv, st_gdn, st_gdn_conv, w_in, ssd_conv_w, ssd_conv_b, ssd_dt_bias,
           ssd_a_log, ssd_d, ssd_norm_w, gdn_conv_w, gdn_dt_bias, gdn_a_log, gdn_norm_w, w_out, ln_g, ln_b,
           *, alpha):
    bsz, seqlen, dm = xp.shape
    ms = xs_in.shape[0]
    mp = bsz * seqlen
    dz = ssd_norm_w.shape[0]
    hs = dz // SSD_HEAD_DIM
    dxbc = ssd_conv_w.shape[1]
    dqkv = gdn_conv_w.shape[1]
    gw = dqkv // 3
    hg = gw // GDN_HEAD_DIM
    assert hs + 2 * hg <= LANES and ms == LANES and seqlen % CHUNK == 0 and mp % ms == 0
    o_dt = dz + dxbc
    o_qkv = o_dt + hs
    o_zg = o_qkv + dqkv
    o_b = o_zg + gw

    w_main = jnp.concatenate([w_in[:, :o_dt], w_in[:, o_qkv:o_b]], axis=1).astype(BF16)
    w_misc = jnp.concatenate(
        [w_in[:, o_dt:o_qkv], w_in[:, o_b:], jnp.zeros((dm, LANES - hs - 2 * hg), w_in.dtype)], axis=1).astype(BF16)
    x_all = jnp.concatenate([xp.reshape(mp, dm), xs_in.reshape(ms, dm)], axis=0).astype(BF16)
    proj, misc = _inproj(x_all, w_main, w_misc)
    q_off = dz + dxbc
    zg_off = q_off + dqkv
    row_blk = mp // ms

    dtb_s = _pad_lanes(ssd_dt_bias, 0)
    alog_s = _pad_lanes(ssd_a_log, 0)
    dtb_g = _pad_lanes(gdn_dt_bias, hs + hg)
    alog_g = _pad_lanes(gdn_a_log, hs + hg)
    dvec = jnp.repeat(ssd_d.astype(F32), SSD_HEAD_DIM)[None, :]
    nw_s = ssd_norm_w.astype(F32)[None, :]
    nw_g = gdn_norm_w.astype(F32)[None, :]
    cb_s = ssd_conv_b.astype(F32)[None, :]

    y_ssd, h_p = _ssd_prompt(proj, misc, ssd_conv_w, cb_s, dtb_s, alog_s, dvec, nw_s,
                             bsz=bsz, seqlen=seqlen, dz=dz)
    y_gdn, s_p = _gdn_prompt(proj, misc, gdn_conv_w, dtb_g, alog_g, nw_g, bsz=bsz, seqlen=seqlen,
                             q_off=q_off, zg_off=zg_off, width=gw, b_lane=hs, a_lane=hs + hg)
    w_out_bf = w_out.astype(BF16)
    g2, b2 = ln_g.astype(F32)[None, :], ln_b.astype(F32)[None, :]
    xp_new = _outproj(y_ssd, y_gdn, w_out_bf, xp.reshape(mp, dm), g2, b2, alpha=alpha).reshape(bsz, seqlen, dm)

    xbc_s = _sample_conv(st_ssd_conv.reshape(ms, -1), proj, ssd_conv_w, cb_s,
                         ms=ms, row_blk=row_blk, col_off=dz, width=dxbc)
    qkv_s = _sample_conv(st_gdn_conv.reshape(ms, -1), proj, gdn_conv_w, jnp.zeros((1, dqkv), F32),
                         ms=ms, row_blk=row_blk, col_off=q_off, width=dqkv)
    dt_s, scal = _sample_scalars(misc, dtb_s, alog_s, dtb_g, alog_g, ms=ms, row_blk=row_blk, hs=hs, hg=hg)
    ys_ssd, h_s = _ssd_sample(scal, st_ssd, xbc_s, dt_s, proj, dvec, nw_s, ms=ms, row_blk=row_blk, dz=dz)
    ys_gdn, s_s = _gdn_sample(scal, st_gdn, qkv_s, proj, nw_g, ms=ms, row_blk=row_blk, hs=hs, zg_off=zg_off)
    xs_new = _outproj(ys_ssd, ys_gdn, w_out_bf, xs_in.reshape(ms, dm), g2, b2, alpha=alpha).reshape(ms, 1, dm)

    raw_p = proj[:mp].reshape(bsz, seqlen, -1)[:, seqlen - (CONV_K - 1):, :]
    conv_ssd_p = raw_p[:, :, dz:dz + dxbc]
    conv_gdn_p = raw_p[:, :, q_off:q_off + dqkv]
    raw_s = proj[mp:]
    conv_ssd_s = jnp.concatenate([st_ssd_conv[:, 1:], raw_s[:, None, dz:dz + dxbc]], axis=1)
    conv_gdn_s = jnp.concatenate([st_gdn_conv[:, 1:], raw_s[:, None, q_off:q_off + dqkv]], axis=1)
    return (xp_new, xs_new, (h_p, conv_ssd_p, s_p, conv_gdn_p), (h_s, conv_ssd_s, s_s, conv_gdn_s))


def kernel(x_prompt, x_sample, state_ssd, state_ssd_conv, state_gdn, state_gdn_conv, w_in, ssd_conv_w, ssd_conv_b,
           ssd_dt_bias, ssd_a_log, ssd_d, ssd_norm_w, gdn_conv_w, gdn_dt_bias, gdn_a_log, gdn_norm_w, w_out,
           ln_g, ln_b):
    depth = w_in.shape[0]
    alpha = (2 * depth) ** 0.25
    assert x_sample.shape[1] == 1
    hp, hs = x_prompt, x_sample
    p_out, s_out = [], []
    for l in range(depth):
        hp, hs, po, so = _layer(
            hp, hs, state_ssd[l], state_ssd_conv[l], state_gdn[l], state_gdn_conv[l], w_in[l], ssd_conv_w[l],
            ssd_conv_b[l], ssd_dt_bias[l], ssd_a_log[l], ssd_d[l], ssd_norm_w[l], gdn_conv_w[l], gdn_dt_bias[l],
            gdn_a_log[l], gdn_norm_w[l], w_out[l], ln_g[l], ln_b[l], alpha=alpha)
        p_out.append(po)
        s_out.append(so)
    stack = lambda outs, i: jnp.stack([o[i] for o in outs])
    return (hp, hs,
            stack(p_out, 0), stack(p_out, 1), stack(p_out, 2), stack(p_out, 3),
            stack(s_out, 0), stack(s_out, 1), stack(s_out, 2), stack(s_out, 3))
```

```python
import functools

import jax
import jax.numpy as jnp
from jax import lax
from jax.experimental import pallas as pl
from jax.experimental.pallas import tpu as pltpu

F32 = jnp.float32
BF16 = jnp.bfloat16

LANES = 128
SUBLANES = 8
SSD_HEAD_DIM = 64
SSD_GROUPS = 8
SSD_STATE = 128
GDN_HEAD_DIM = 128
CONV_K = 4
NORM_EPS = 1e-6
LN_EPS = 1e-5
CHUNK = 128
NEG_BIG = -1e30
VMEM_LIMIT = 56 * 1024 * 1024


def _divisor_tile(n, target, mult):
    best = None
    t = mult
    while t <= min(n, target):
        if n % t == 0:
            best = t
        t += mult
    assert best is not None, (n, target, mult)
    return best


def _dot(a, b):
    return jnp.dot(a, b, preferred_element_type=F32)


def _dot_nt(a, b):
    return lax.dot_general(a, b, (((1,), (1,)), ((), ())), preferred_element_type=F32)


def _split3(x):
    x1 = x.astype(BF16)
    r1 = x - x1.astype(F32)
    x2 = r1.astype(BF16)
    r2 = r1 - x2.astype(F32)
    return x1, x2, r2.astype(BF16)


def _dot_sel_lhs(sel_bf, x):
    x1, x2, x3 = _split3(x)
    return _dot(sel_bf, x1) + _dot(sel_bf, x2) + _dot(sel_bf, x3)


def _dot_sel_rhs(x, sel_bf):
    x1, x2, x3 = _split3(x)
    return _dot(x1, sel_bf) + _dot(x2, sel_bf) + _dot(x3, sel_bf)


def _sigmoid(x):
    return 1.0 / (1.0 + jnp.exp(-x))


def _silu(x):
    return x * _sigmoid(x)


def _softplus(x):
    return jnp.maximum(x, 0.0) + jnp.log1p(jnp.exp(-jnp.abs(x)))


def _tril01(t, dtype=BF16):
    r = lax.broadcasted_iota(jnp.int32, (t, t), 0)
    c = lax.broadcasted_iota(jnp.int32, (t, t), 1)
    return (r >= c).astype(dtype)


def _expander(rows, cols, per, row_offset=0):
    assert per & (per - 1) == 0
    r = lax.broadcasted_iota(jnp.int32, (rows, cols), 0)
    c = lax.broadcasted_iota(jnp.int32, (rows, cols), 1)
    return (lax.shift_right_logical(c, per.bit_length() - 1) + row_offset == r).astype(BF16)


def _causal_conv_silu(raw_ref, ext_ref, w_ref, bias, t):
    ext_ref[SUBLANES:SUBLANES + t, :] = raw_ref[...]
    acc = ext_ref[SUBLANES - 3:SUBLANES - 3 + t, :] * w_ref[0:1, :]
    if bias is not None:
        acc = acc + bias
    for j in range(1, CONV_K):
        acc = acc + ext_ref[SUBLANES - 3 + j:SUBLANES - 3 + j + t, :] * w_ref[j:j + 1, :]
    ext_ref[0:SUBLANES, :] = ext_ref[t:t + SUBLANES, :]
    return _silu(acc)


def _lane_roll(x, shift):
    amount = jnp.where(shift == 0, 0, LANES - shift)
    return pltpu.roll(x, amount, 1)


def _inproj_kernel(xp_ref, xs_ref, wa_ref, wb_ref, wm_ref, o_ref, om_ref, *, npt, na, ms):
    i = pl.program_id(0)
    j = pl.program_id(1)

    def emit(x_ref, rows):
        @pl.when(j < na)
        def _():
            o_ref[0:rows, :] = _dot(x_ref[...], wa_ref[...])

        @pl.when(j >= na)
        def _():
            o_ref[0:rows, :] = _dot(x_ref[...], wb_ref[...])

        @pl.when(j == 0)
        def _():
            om_ref[0:rows, :] = _dot(x_ref[...], wm_ref[...])

    @pl.when(i < npt)
    def _():
        emit(xp_ref, xp_ref.shape[0])

    @pl.when(i == npt)
    def _():
        emit(xs_ref, ms)


def _inproj(xp_bf, xs_bf, w_a, w_b, w_misc):
    mp, k = xp_bf.shape
    ms = xs_bf.shape[0]
    n_a, n_b = w_a.shape[1], w_b.shape[1]
    nm = w_misc.shape[1]
    tm = _divisor_tile(mp, 1024, 16)
    assert ms <= tm
    tn = 512
    assert n_a % tn == 0 and n_b % tn == 0
    npt, na, nb = mp // tm, n_a // tn, n_b // tn
    m, n = mp + ms, n_a + n_b
    return pl.pallas_call(
        functools.partial(_inproj_kernel, npt=npt, na=na, ms=ms),
        grid=(npt + 1, na + nb),
        in_specs=[
            pl.BlockSpec((tm, k), lambda i, j: (jnp.minimum(i, npt - 1), 0)),
            pl.BlockSpec((ms, k), lambda i, j: (0, 0)),
            pl.BlockSpec((k, tn), lambda i, j: (0, jnp.minimum(j, na - 1))),
            pl.BlockSpec((k, tn), lambda i, j: (0, jnp.maximum(j - na, 0))),
            pl.BlockSpec((k, nm), lambda i, j: (0, 0)),
        ],
        out_specs=[
            pl.BlockSpec((tm, tn), lambda i, j: (i, j)),
            pl.BlockSpec((tm, nm), lambda i, j: (i, 0)),
        ],
        out_shape=[jax.ShapeDtypeStruct((m, n), F32), jax.ShapeDtypeStruct((m, nm), F32)],
        compiler_params=pltpu.CompilerParams(
            dimension_semantics=("parallel", "arbitrary"), vmem_limit_bytes=VMEM_LIMIT),
        name="inproj",
    )(xp_bf, xs_bf, w_a, w_b, w_misc)


def _ssd_kernel(z_ref, xs_ref, b_ref, c_ref, misc_ref,
                wx_ref, wb_ref, wc_ref, bx_ref, bb_ref, bc_ref,
                dtb_ref, alog_ref, dvec_ref, nw_ref,
                y_ref, hout_ref,
                ht_ref, extx_ref, extb_ref, extc_ref, *, t, hpg, nc):
    g = pl.program_id(1)
    c = pl.program_id(2)
    wg = hpg * SSD_HEAD_DIM

    @pl.when(c == 0)
    def _():
        ht_ref[...] = jnp.zeros_like(ht_ref)
        extx_ref[0:SUBLANES, :] = jnp.zeros((SUBLANES, wg), F32)
        extb_ref[0:SUBLANES, :] = jnp.zeros((SUBLANES, SSD_STATE), F32)
        extc_ref[0:SUBLANES, :] = jnp.zeros((SUBLANES, SSD_STATE), F32)

    xs = _causal_conv_silu(xs_ref, extx_ref, wx_ref, bx_ref[...], t)
    bm = _causal_conv_silu(b_ref, extb_ref, wb_ref, bb_ref[...], t)
    cm = _causal_conv_silu(c_ref, extc_ref, wc_ref, bc_ref[...], t)

    misc = misc_ref[...]
    dt_all = _softplus(misc + dtb_ref[...])
    da_all = dt_all * (-jnp.exp(alog_ref[...]))
    dt_r = _lane_roll(dt_all, g * hpg)
    da_r = _lane_roll(da_all, g * hpg)

    tril = _tril01(t)
    cum = _dot_sel_lhs(tril, da_r)
    cum_t = cum.T
    expand = _expander(LANES, wg, SSD_HEAD_DIM)
    cum_b = _dot_sel_rhs(cum, expand)
    dt_b = _dot_sel_rhs(dt_r, expand)
    last_b = cum_b[t - 1:t, :]
    xdt = xs * dt_b

    bm_bf = bm.astype(BF16)
    cm_bf = cm.astype(BF16)
    cb = _dot_nt(cm_bf, bm_bf)
    rows = lax.broadcasted_iota(jnp.int32, (t, t), 0)
    cols = lax.broadcasted_iota(jnp.int32, (t, t), 1)
    causal = rows >= cols
    lane = lax.broadcasted_iota(jnp.int32, (t, LANES), 1)

    ys = []
    for p in range(hpg // 2):
        wms = []
        for h in (2 * p, 2 * p + 1):
            col = jnp.broadcast_to(cum[:, h:h + 1], (t, t))
            row = jnp.broadcast_to(cum_t[h:h + 1, :], (t, t))
            dec = jnp.exp(jnp.where(causal, col - row, NEG_BIG))
            wms.append((cb * dec).astype(BF16))
        lhs = jnp.concatenate(wms, axis=1)
        xp = xdt[:, p * LANES:(p + 1) * LANES]
        rhs = jnp.concatenate([jnp.where(lane < SSD_HEAD_DIM, xp, 0.0),
                               jnp.where(lane >= SSD_HEAD_DIM, xp, 0.0)], axis=0).astype(BF16)
        ys.append(_dot(lhs, rhs))
    y = jnp.concatenate(ys, axis=1) if len(ys) > 1 else ys[0]

    ht = ht_ref[...]
    y = y + _dot(cm_bf, ht.astype(BF16)) * jnp.exp(cum_b)
    y = y + xs * dvec_ref[...]
    u = y * _silu(z_ref[...])
    ms = jnp.mean(u * u, axis=-1, keepdims=True)
    y_ref[...] = (u * lax.rsqrt(ms + NORM_EPS) * nw_ref[...]).astype(y_ref.dtype)

    w_s = jnp.exp(last_b - cum_b)
    ht_new = ht * jnp.exp(last_b) + _dot(bm.T.astype(BF16), (xdt * w_s).astype(BF16))
    ht_ref[...] = ht_new

    @pl.when(c == nc - 1)
    def _():
        hout_ref[0] = ht_new.T.reshape(hpg, SSD_HEAD_DIM, SSD_STATE)


def _ssd_prompt(proj, misc, conv_w, conv_b, dtb_pad, alog_pad, dvec, norm_w, *, bsz, seqlen, dz):
    t = CHUNK
    nc = seqlen // t
    hs = dz // SSD_HEAD_DIM
    hpg = hs // SSD_GROUPS
    wg = hpg * SSD_HEAD_DIM
    ng = SSD_GROUPS
    xs_blk = dz // wg
    b_blk = 2 * dz // SSD_STATE
    c_blk = b_blk + ng
    cwb = dz // SSD_STATE
    row = lambda b, g, c: b * nc + c
    kern = functools.partial(_ssd_kernel, t=t, hpg=hpg, nc=nc)
    return pl.pallas_call(
        kern,
        grid=(bsz, ng, nc),
        in_specs=[
            pl.BlockSpec((t, wg), lambda b, g, c: (row(b, g, c), g)),
            pl.BlockSpec((t, wg), lambda b, g, c: (row(b, g, c), xs_blk + g)),
            pl.BlockSpec((t, SSD_STATE), lambda b, g, c: (row(b, g, c), b_blk + g)),
            pl.BlockSpec((t, SSD_STATE), lambda b, g, c: (row(b, g, c), c_blk + g)),
            pl.BlockSpec((t, LANES), lambda b, g, c: (row(b, g, c), 0)),
            pl.BlockSpec((CONV_K, wg), lambda b, g, c: (0, g)),
            pl.BlockSpec((CONV_K, SSD_STATE), lambda b, g, c: (0, cwb + g)),
            pl.BlockSpec((CONV_K, SSD_STATE), lambda b, g, c: (0, cwb + ng + g)),
            pl.BlockSpec((1, wg), lambda b, g, c: (0, g)),
            pl.BlockSpec((1, SSD_STATE), lambda b, g, c: (0, cwb + g)),
            pl.BlockSpec((1, SSD_STATE), lambda b, g, c: (0, cwb + ng + g)),
            pl.BlockSpec((1, LANES), lambda b, g, c: (0, 0)),
            pl.BlockSpec((1, LANES), lambda b, g, c: (0, 0)),
            pl.BlockSpec((1, wg), lambda b, g, c: (0, g)),
            pl.BlockSpec((1, wg), lambda b, g, c: (0, g)),
        ],
        out_specs=[
            pl.BlockSpec((t, wg), lambda b, g, c: (row(b, g, c), g)),
            pl.BlockSpec((1, hpg, SSD_HEAD_DIM, SSD_STATE), lambda b, g, c: (b, g, 0, 0)),
        ],
        out_shape=[
            jax.ShapeDtypeStruct((bsz * seqlen, dz), BF16),
            jax.ShapeDtypeStruct((bsz, hs, SSD_HEAD_DIM, SSD_STATE), F32),
        ],
        scratch_shapes=[
            pltpu.VMEM((SSD_STATE, wg), F32),
            pltpu.VMEM((t + SUBLANES, wg), F32),
            pltpu.VMEM((t + SUBLANES, SSD_STATE), F32),
            pltpu.VMEM((t + SUBLANES, SSD_STATE), F32),
        ],
        compiler_params=pltpu.CompilerParams(
            dimension_semantics=("parallel", "parallel", "arbitrary"), vmem_limit_bytes=VMEM_LIMIT),
        name="ssd_prompt",
    )(proj, proj, proj, proj, misc, conv_w, conv_w, conv_w, conv_b, conv_b, conv_b,
      dtb_pad, alog_pad, dvec, norm_w)


INV_BASE = 16
GDN_HEADS_PER_STEP = 8


def _inverse_masks(t):
    rows = lax.broadcasted_iota(jnp.int32, (t, t), 0)
    cols = lax.broadcasted_iota(jnp.int32, (t, t), 1)
    eye = jnp.where(rows == cols, 1.0, 0.0)
    shift = INV_BASE.bit_length() - 1
    same = lax.shift_right_logical(rows, shift) == lax.shift_right_logical(cols, shift)
    masks = [same]
    b = INV_BASE
    while b < t:
        shift += 1
        same2 = lax.shift_right_logical(rows, shift) == lax.shift_right_logical(cols, shift)
        masks.append(jnp.logical_and(same2, jnp.logical_not(same)))
        same = same2
        b *= 2
    return eye, masks


def _unit_lower_inverse(ns, eye, masks):
    heads = range(len(ns))
    nks = [jnp.where(masks[0], n, 0.0) for n in ns]
    qs = [-nk for nk in nks]
    k = 2
    while k < INV_BASE:
        nk_bfs = [nk.astype(BF16) for nk in nks]
        nks = [_dot(nk_bf, nk_bf) for nk_bf in nk_bfs]
        upd = [_dot(qs[i].astype(BF16), nks[i].astype(BF16)) for i in heads]
        qs = [qs[i] + nks[i] + upd[i] for i in heads]
        k *= 2
    ds = [eye + q for q in qs]
    for off in masks[1:]:
        d_bfs = [d.astype(BF16) for d in ds]
        dc = [_dot(d_bfs[i], jnp.where(off, ns[i], 0.0).astype(BF16)) for i in heads]
        dcd = [_dot(dc[i].astype(BF16), d_bfs[i]) for i in heads]
        ds = [ds[i] - dcd[i] for i in heads]
    return ds


def _gdn_kernel(q_ref, k_ref, v_ref, zg_ref, misc_ref,
                wq_ref, wk_ref, wv_ref, dtb_ref, alog_ref, nw_ref,
                y_ref, sout_ref,
                s_ref, extq_ref, extk_ref, extv_ref, *, t, hb, nc, b_lane, a_lane):
    hblk = pl.program_id(1)
    c = pl.program_id(2)
    d = GDN_HEAD_DIM

    @pl.when(c == 0)
    def _():
        s_ref[...] = jnp.zeros_like(s_ref)
        for ext in (extq_ref, extk_ref, extv_ref):
            ext[0:SUBLANES, :] = jnp.zeros((SUBLANES, hb * d), F32)

    q = _causal_conv_silu(q_ref, extq_ref, wq_ref, None, t)
    k = _causal_conv_silu(k_ref, extk_ref, wk_ref, None, t)
    v = _causal_conv_silu(v_ref, extv_ref, wv_ref, None, t)
    zg = zg_ref[...]

    misc = misc_ref[...]
    beta_r = _lane_roll(_sigmoid(misc), b_lane + hblk * hb)
    g_all = -jnp.exp(alog_ref[...]) * _softplus(misc + dtb_ref[...])
    g_r = _lane_roll(g_all, a_lane + hblk * hb)

    tril = _tril01(t)
    cum = _dot_sel_lhs(tril, g_r)
    cum_t = cum.T
    ecum = jnp.exp(cum)
    last = cum[t - 1:t, :]
    wdec = jnp.exp(last - cum)
    elast = jnp.exp(last)

    rows = lax.broadcasted_iota(jnp.int32, (t, t), 0)
    cols = lax.broadcasted_iota(jnp.int32, (t, t), 1)
    causal = rows >= cols
    strict = rows > cols
    eye, inv_masks = _inverse_masks(t)

    heads = range(hb)
    sls = [slice(i * d, (i + 1) * d) for i in heads]
    colb = lambda a, i, w: jnp.broadcast_to(a[:, i:i + 1], (t, w))
    qn = [q[:, sl] * lax.rsqrt(jnp.sum(q[:, sl] * q[:, sl], axis=-1, keepdims=True) + NORM_EPS) * (d ** -0.5)
          for sl in sls]
    kn = [k[:, sl] * lax.rsqrt(jnp.sum(k[:, sl] * k[:, sl], axis=-1, keepdims=True) + NORM_EPS) for sl in sls]
    dec = [jnp.exp(jnp.where(causal, colb(cum, i, t) - jnp.broadcast_to(cum_t[i:i + 1, :], (t, t)), NEG_BIG))
           for i in heads]
    kn_bf = [x.astype(BF16) for x in kn]
    qn_bf = [x.astype(BF16) for x in qn]
    kk = [_dot_nt(kn_bf[i], kn_bf[i]) for i in heads]
    qk = [_dot_nt(qn_bf[i], kn_bf[i]) for i in heads]
    ns = [jnp.where(strict, kk[i] * dec[i] * colb(beta_r, i, t), 0.0) for i in heads]
    minv = _unit_lower_inverse(ns, eye, inv_masks)
    rhs = [jnp.concatenate([v[:, sls[i]] * colb(beta_r, i, d), kn[i] * colb(beta_r, i, d) * colb(ecum, i, d)],
                           axis=1).astype(BF16) for i in heads]
    x = [_dot(minv[i].astype(BF16), rhs[i]) for i in heads]
    s_old = [s_ref[i] for i in heads]
    r = [_dot(jnp.concatenate([x[i][:, d:], qn[i] * colb(ecum, i, d)], axis=0).astype(BF16),
              s_old[i].astype(BF16)) for i in heads]
    v_new_bf = [(x[i][:, :d] - r[i][:t]).astype(BF16) for i in heads]
    o = [_dot((qk[i] * dec[i]).astype(BF16), v_new_bf[i]) for i in heads]
    upd = [_dot((kn[i] * colb(wdec, i, d)).T.astype(BF16), v_new_bf[i]) for i in heads]
    outs = []
    for i in heads:
        s_ref[i] = s_old[i] * elast[:, i:i + 1] + upd[i]
        oi = r[i][t:] + o[i]
        ms = jnp.mean(oi * oi, axis=-1, keepdims=True)
        outs.append(oi * lax.rsqrt(ms + NORM_EPS) * nw_ref[...] * _silu(zg[:, sls[i]]))
    y = jnp.concatenate(outs, axis=1) if hb > 1 else outs[0]
    y_ref[...] = y.astype(y_ref.dtype)

    @pl.when(c == nc - 1)
    def _():
        sout_ref[0] = s_ref[...]


def _gdn_prompt(proj, misc, conv_w, dtb_pad, alog_pad, norm_w, *, bsz, seqlen, q_off, zg_off, width,
                b_lane, a_lane):
    t = CHUNK
    nc = seqlen // t
    d = GDN_HEAD_DIM
    hg = width // d
    hb = _divisor_tile(hg, GDN_HEADS_PER_STEP, 1)
    bw = hb * d
    nhb = hg // hb
    qb, kb, vb, zb = q_off // bw, (q_off + width) // bw, (q_off + 2 * width) // bw, zg_off // bw
    wkb, wvb = width // bw, 2 * width // bw
    row = lambda b, h, c: b * nc + c
    kern = functools.partial(_gdn_kernel, t=t, hb=hb, nc=nc, b_lane=b_lane, a_lane=a_lane)
    return pl.pallas_call(
        kern,
        grid=(bsz, nhb, nc),
        in_specs=[
            pl.BlockSpec((t, bw), lambda b, h, c: (row(b, h, c), qb + h)),
            pl.BlockSpec((t, bw), lambda b, h, c: (row(b, h, c), kb + h)),
            pl.BlockSpec((t, bw), lambda b, h, c: (row(b, h, c), vb + h)),
            pl.BlockSpec((t, bw), lambda b, h, c: (row(b, h, c), zb + h)),
            pl.BlockSpec((t, LANES), lambda b, h, c: (row(b, h, c), 0)),
            pl.BlockSpec((CONV_K, bw), lambda b, h, c: (0, h)),
            pl.BlockSpec((CONV_K, bw), lambda b, h, c: (0, wkb + h)),
            pl.BlockSpec((CONV_K, bw), lambda b, h, c: (0, wvb + h)),
            pl.BlockSpec((1, LANES), lambda b, h, c: (0, 0)),
            pl.BlockSpec((1, LANES), lambda b, h, c: (0, 0)),
            pl.BlockSpec((1, d), lambda b, h, c: (0, 0)),
        ],
        out_specs=[
            pl.BlockSpec((t, bw), lambda b, h, c: (row(b, h, c), h)),
            pl.BlockSpec((1, hb, d, d), lambda b, h, c: (b, h, 0, 0)),
        ],
        out_shape=[
            jax.ShapeDtypeStruct((bsz * seqlen, width), BF16),
            jax.ShapeDtypeStruct((bsz, hg, d, d), F32),
        ],
        scratch_shapes=[
            pltpu.VMEM((hb, d, d), F32),
            pltpu.VMEM((t + SUBLANES, bw), F32),
            pltpu.VMEM((t + SUBLANES, bw), F32),
            pltpu.VMEM((t + SUBLANES, bw), F32),
        ],
        compiler_params=pltpu.CompilerParams(
            dimension_semantics=("parallel", "parallel", "arbitrary"), vmem_limit_bytes=VMEM_LIMIT),
        name="gdn_prompt",
    )(proj, proj, proj, proj, misc, conv_w, conv_w, conv_w, dtb_pad, alog_pad, norm_w)


def _sample_conv_kernel(s0_ref, s1_ref, s2_ref, u_ref, w_ref, b_ref, o_ref):
    acc = b_ref[...] + s0_ref[...] * w_ref[0:1, :]
    acc = acc + s1_ref[...] * w_ref[1:2, :]
    acc = acc + s2_ref[...] * w_ref[2:3, :]
    acc = acc + u_ref[...] * w_ref[3:4, :]
    o_ref[...] = _silu(acc)


def _sample_conv(state2d, proj, conv_w, conv_b, *, ms, row_blk, col_off, width):
    wb = _divisor_tile(width, 512, LANES)
    nb = width // wb
    cb = col_off // wb
    return pl.pallas_call(
        _sample_conv_kernel,
        grid=(nb,),
        in_specs=[
            pl.BlockSpec((ms, wb), lambda j: (0, j)),
            pl.BlockSpec((ms, wb), lambda j: (0, nb + j)),
            pl.BlockSpec((ms, wb), lambda j: (0, 2 * nb + j)),
            pl.BlockSpec((ms, wb), lambda j: (row_blk, cb + j)),
            pl.BlockSpec((CONV_K, wb), lambda j: (0, j)),
            pl.BlockSpec((1, wb), lambda j: (0, j)),
        ],
        out_specs=pl.BlockSpec((ms, wb), lambda j: (0, j)),
        out_shape=jax.ShapeDtypeStruct((ms, width), F32),
        compiler_params=pltpu.CompilerParams(dimension_semantics=("parallel",)),
        name="sample_conv",
    )(state2d, state2d, state2d, proj, conv_w, conv_b)


def _sample_scalars_kernel(misc_ref, dtb_ref, alog_s_ref, dtb_g_ref, alog_g_ref, dt_ref, sc_ref, *, hs, hg):
    misc = misc_ref[...]
    lane = lax.broadcasted_iota(jnp.int32, misc.shape, 1)
    dt = _softplus(misc + dtb_ref[...])
    eda = jnp.exp(dt * (-jnp.exp(alog_s_ref[...])))
    beta = _sigmoid(misc)
    eg = jnp.exp(-jnp.exp(alog_g_ref[...]) * _softplus(misc + dtb_g_ref[...]))
    dt_ref[...] = dt
    sc_ref[...] = jnp.where(lane < hs, eda, jnp.where(lane < hs + hg, beta, eg))


def _sample_scalars(misc, dtb_s, alog_s, dtb_g, alog_g, *, ms, row_blk, hs, hg):
    vec = pl.BlockSpec((1, LANES), lambda i: (0, 0))
    return pl.pallas_call(
        functools.partial(_sample_scalars_kernel, hs=hs, hg=hg),
        grid=(1,),
        in_specs=[pl.BlockSpec((ms, LANES), lambda i: (row_blk, 0)), vec, vec, vec, vec],
        out_specs=[pl.BlockSpec((ms, LANES), lambda i: (0, 0)), pl.BlockSpec((ms, LANES), lambda i: (0, 0))],
        out_shape=[jax.ShapeDtypeStruct((ms, LANES), F32), jax.ShapeDtypeStruct((ms, LANES), F32)],
        name="sample_scalars",
    )(misc, dtb_s, alog_s, dtb_g, alog_g)


def _transpose_blocks(x, nblk):
    return jnp.concatenate([x[:, j * LANES:(j + 1) * LANES].T for j in range(nblk)], axis=0)


def _row_group(b):
    b8 = pl.multiple_of(lax.shift_right_logical(b, 3) * SUBLANES, SUBLANES)
    in_group = lax.broadcasted_iota(jnp.int32, (SUBLANES, LANES), 0) == jnp.bitwise_and(b, SUBLANES - 1)
    return b8, in_group


def _untranspose_blocks(x, nblk):
    return jnp.concatenate([x[j * LANES:(j + 1) * LANES, :].T for j in range(nblk)], axis=1)


def _ssd_sample_kernel(sc_smem, h_ref, xbc_ref, dt_ref, z_ref, dvec_ref, nw_ref,
                       y_ref, hout_ref,
                       xt_ref, yt_ref, *, ms, hs, dz):
    b = pl.program_id(0)
    hpg = hs // SSD_GROUPS
    wg = hpg * SSD_HEAD_DIM

    @pl.when(b == 0)
    def _():
        xs = xbc_ref[:, 0:dz]
        dt_b = _dot_sel_rhs(dt_ref[...], _expander(LANES, dz, SSD_HEAD_DIM))
        xt_ref[...] = _transpose_blocks(xs * dt_b, dz // LANES)
        yt_ref[...] = jnp.zeros_like(yt_ref)

    seq_rows = lax.broadcasted_iota(jnp.int32, (ms, SSD_STATE), 0)
    seq_lanes = lax.broadcasted_iota(jnp.int32, (wg, LANES), 1)
    b8, in_group = _row_group(b)
    for g in range(SSD_GROUPS):
        hgrp = h_ref[0, g * hpg:(g + 1) * hpg].reshape(wg, SSD_STATE)
        bm = xbc_ref[:, dz + g * SSD_STATE: dz + (g + 1) * SSD_STATE]
        bm = jnp.where(seq_rows == b, bm, 0.0).astype(BF16)
        upd = _dot(xt_ref[g * wg:(g + 1) * wg, :].astype(BF16), bm)
        decay = jnp.concatenate(
            [jnp.full((SSD_HEAD_DIM, SSD_STATE), sc_smem[b, g * hpg + h], F32) for h in range(hpg)], axis=0)
        hn = hgrp * decay + upd
        hout_ref[0, g * hpg:(g + 1) * hpg] = hn.reshape(hpg, SSD_HEAD_DIM, SSD_STATE)
        c_off = dz + SSD_GROUPS * SSD_STATE + g * SSD_STATE
        crow = jnp.sum(jnp.where(in_group, xbc_ref[pl.ds(b8, SUBLANES), c_off:c_off + SSD_STATE], 0.0),
                       axis=0, keepdims=True)
        ycol = jnp.sum(hn * crow, axis=1, keepdims=True)
        cur = yt_ref[g * wg:(g + 1) * wg, :]
        yt_ref[g * wg:(g + 1) * wg, :] = jnp.where(seq_lanes == b, ycol, cur)

    @pl.when(b == ms - 1)
    def _():
        y = _untranspose_blocks(yt_ref[...], dz // LANES)
        y = y + xbc_ref[:, 0:dz] * dvec_ref[...]
        u = y * _silu(z_ref[...])
        outs = []
        for g in range(SSD_GROUPS):
            ug = u[:, g * wg:(g + 1) * wg]
            msq = jnp.mean(ug * ug, axis=-1, keepdims=True)
            outs.append(ug * lax.rsqrt(msq + NORM_EPS))
        y_ref[...] = (jnp.concatenate(outs, axis=1) * nw_ref[...]).astype(y_ref.dtype)


def _ssd_sample(scal, h_state, xbc_s, dt_s, proj, dvec, norm_w, *, ms, row_blk, dz):
    hs = dz // SSD_HEAD_DIM
    cw = xbc_s.shape[1]
    full = lambda shape: pl.BlockSpec(shape, lambda b: tuple(0 for _ in shape))
    return pl.pallas_call(
        functools.partial(_ssd_sample_kernel, ms=ms, hs=hs, dz=dz),
        grid=(ms,),
        in_specs=[
            pl.BlockSpec(memory_space=pltpu.SMEM),
            pl.BlockSpec((1, hs, SSD_HEAD_DIM, SSD_STATE), lambda b: (b, 0, 0, 0)),
            full((ms, cw)),
            full((ms, LANES)),
            pl.BlockSpec((ms, dz), lambda b: (row_blk, 0)),
            full((1, dz)),
            full((1, dz)),
        ],
        out_specs=[
            full((ms, dz)),
            pl.BlockSpec((1, hs, SSD_HEAD_DIM, SSD_STATE), lambda b: (b, 0, 0, 0)),
        ],
        out_shape=[
            jax.ShapeDtypeStruct((ms, dz), BF16),
            jax.ShapeDtypeStruct(h_state.shape, F32),
        ],
        scratch_shapes=[pltpu.VMEM((dz, LANES), F32), pltpu.VMEM((dz, LANES), F32)],
        compiler_params=pltpu.CompilerParams(
            dimension_semantics=("arbitrary",), vmem_limit_bytes=VMEM_LIMIT),
        name="ssd_sample",
    )(scal, h_state, xbc_s, dt_s, proj, dvec, norm_w)


def _gdn_sample_kernel(sc_smem, s_ref, qkv_ref, sc_ref, zg0_ref, zg1_ref, nw_ref,
                       y_ref, sout_ref,
                       w_ref, qe_ref, vb_ref, qk_ref, kt_ref, o_ref, *, ms, hs, hg):
    b = pl.program_id(0)
    d = GDN_HEAD_DIM
    width = hg * d

    @pl.when(b == 0)
    def _():
        sc = sc_ref[...]
        beta_b = _dot_sel_rhs(sc, _expander(LANES, width, d, hs))
        eg_b = _dot_sel_rhs(sc, _expander(LANES, width, d, hs + hg))
        for h in range(hg):
            sl = slice(h * d, (h + 1) * d)
            qh = qkv_ref[:, h * d:(h + 1) * d]
            kh = qkv_ref[:, width + h * d: width + (h + 1) * d]
            vh = qkv_ref[:, 2 * width + h * d: 2 * width + (h + 1) * d]
            qn = qh * lax.rsqrt(jnp.sum(qh * qh, axis=-1, keepdims=True) + NORM_EPS) * (d ** -0.5)
            kn = kh * lax.rsqrt(jnp.sum(kh * kh, axis=-1, keepdims=True) + NORM_EPS)
            w_ref[:, sl] = kn * beta_b[:, sl] * eg_b[:, sl]
            qe_ref[:, sl] = qn * eg_b[:, sl]
            vb_ref[:, sl] = vh * beta_b[:, sl]
            qk_ref[:, sl] = jnp.broadcast_to(jnp.sum(qn * kn, axis=-1, keepdims=True), (ms, d))
            kt_ref[sl, :] = kn.T

    seq_rows = lax.broadcasted_iota(jnp.int32, (ms, d), 0)
    b8, in_group = _row_group(b)
    for h in range(hg):
        sl = slice(h * d, (h + 1) * d)
        s = s_ref[0, h]
        w8 = jnp.where(in_group, w_ref[pl.ds(b8, SUBLANES), sl], 0.0)
        q8 = jnp.where(in_group, qe_ref[pl.ds(b8, SUBLANES), sl], 0.0)
        r = _dot(jnp.concatenate([w8, q8], axis=0).astype(BF16), s.astype(BF16))
        vb8 = jnp.where(in_group, vb_ref[pl.ds(b8, SUBLANES), sl], 0.0)
        v_new8 = vb8 - r[0:SUBLANES]
        v_new = jnp.sum(v_new8, axis=0, keepdims=True)
        o8 = r[SUBLANES:] + qk_ref[pl.ds(b8, SUBLANES), sl] * v_new8
        o_ref[pl.ds(b8, SUBLANES), sl] = jnp.where(in_group, o8, o_ref[pl.ds(b8, SUBLANES), sl])
        vm = jnp.where(seq_rows == b, jnp.broadcast_to(v_new, (ms, d)), 0.0).astype(BF16)
        s_new = s * sc_smem[b, hs + hg + h] + _dot(kt_ref[sl, :].astype(BF16), vm)
        sout_ref[0, h] = s_new

    @pl.when(b == ms - 1)
    def _():
        outs = []
        for h in range(hg):
            sl = slice(h * d, (h + 1) * d)
            o = o_ref[:, sl]
            msq = jnp.mean(o * o, axis=-1, keepdims=True)
            zg_ref, zoff = (zg0_ref, 0) if h < hg // 2 else (zg1_ref, width // 2)
            zg = zg_ref[:, h * d - zoff:(h + 1) * d - zoff]
            outs.append(o * lax.rsqrt(msq + NORM_EPS) * nw_ref[...] * _silu(zg))
        y_ref[...] = jnp.concatenate(outs, axis=1).astype(y_ref.dtype)


def _gdn_sample(scal, s_state, qkv_s, proj, norm_w, *, ms, row_blk, hs, zg_off):
    d = GDN_HEAD_DIM
    hg = s_state.shape[1]
    width = hg * d
    full = lambda shape: pl.BlockSpec(shape, lambda b: tuple(0 for _ in shape))
    act = pltpu.VMEM((ms, width), F32)
    return pl.pallas_call(
        functools.partial(_gdn_sample_kernel, ms=ms, hs=hs, hg=hg),
        grid=(ms,),
        in_specs=[
            pl.BlockSpec(memory_space=pltpu.SMEM),
            pl.BlockSpec((1, hg, d, d), lambda b: (b, 0, 0, 0)),
            full((ms, 3 * width)),
            full((ms, LANES)),
            pl.BlockSpec((ms, width // 2), lambda b: (row_blk, zg_off // (width // 2))),
            pl.BlockSpec((ms, width // 2), lambda b: (row_blk, zg_off // (width // 2) + 1)),
            full((1, d)),
        ],
        out_specs=[
            full((ms, width)),
            pl.BlockSpec((1, hg, d, d), lambda b: (b, 0, 0, 0)),
        ],
        out_shape=[
            jax.ShapeDtypeStruct((ms, width), BF16),
            jax.ShapeDtypeStruct(s_state.shape, F32),
        ],
        scratch_shapes=[act, act, act, act, pltpu.VMEM((width, LANES), F32), act],
        compiler_params=pltpu.CompilerParams(
            dimension_semantics=("arbitrary",), vmem_limit_bytes=VMEM_LIMIT),
        name="gdn_sample",
    )(scal, s_state, qkv_s, scal, proj, proj, norm_w)


def _outproj_kernel(ya_ref, yb_ref, w_ref, x_ref, g_ref, b_ref, o_ref, *, nkh, nk, alpha):
    k = pl.program_id(1)

    @pl.when(k == 0)
    def _():
        o_ref[...] = jnp.zeros_like(o_ref)

    @pl.when(k < nkh)
    def _():
        o_ref[...] += _dot(ya_ref[...], w_ref[...])

    @pl.when(k >= nkh)
    def _():
        o_ref[...] += _dot(yb_ref[...], w_ref[...])

    @pl.when(k == nk - 1)
    def _():
        y = alpha * x_ref[...] + o_ref[...]
        mu = jnp.mean(y, axis=-1, keepdims=True)
        yc = y - mu
        var = jnp.mean(yc * yc, axis=-1, keepdims=True)
        o_ref[...] = yc * lax.rsqrt(var + LN_EPS) * g_ref[...] + b_ref[...]


def _outproj(ya, yb, w_bf, x, ln_g, ln_b, *, alpha):
    m, ka = ya.shape
    kb = yb.shape[1]
    d = w_bf.shape[1]
    tm = _divisor_tile(m, 512, SUBLANES)
    tk = _divisor_tile(min(ka, kb), 512, LANES)
    assert ka % tk == 0 and kb % tk == 0
    nkh = ka // tk
    nk = nkh + kb // tk
    return pl.pallas_call(
        functools.partial(_outproj_kernel, nkh=nkh, nk=nk, alpha=alpha),
        grid=(m // tm, nk),
        in_specs=[
            pl.BlockSpec((tm, tk), lambda i, k: (i, jnp.minimum(k, nkh - 1))),
            pl.BlockSpec((tm, tk), lambda i, k: (i, jnp.maximum(k - nkh, 0))),
            pl.BlockSpec((tk, d), lambda i, k: (k, 0)),
            pl.BlockSpec((tm, d), lambda i, k: (i, 0)),
            pl.BlockSpec((1, d), lambda i, k: (0, 0)),
            pl.BlockSpec((1, d), lambda i, k: (0, 0)),
        ],
        out_specs=pl.BlockSpec((tm, d), lambda i, k: (i, 0)),
        out_shape=jax.ShapeDtypeStruct((m, d), F32),
        compiler_params=pltpu.CompilerParams(
            dimension_semantics=("parallel", "arbitrary"), vmem_limit_bytes=VMEM_LIMIT),
        name="outproj",
    )(ya, yb, w_bf, x, ln_g, ln_b)


def _pad_lanes(vec, offset):
    out = jnp.zeros((1, LANES), F32)
    return lax.dynamic_update_slice(out, vec.astype(F32)[None, :], (0, offset))


def _layer(xp, xs_in, st_ssd, st_ssd_conv, st_gdn, st_gdn_conv, w_in, ssd_conv_w, ssd_conv_b, ssd_dt_bias,
           ssd_a_log, ssd_d, ssd_norm_w, gdn_conv_w, gdn_dt_bias, gdn_a_log, gdn_norm_w, w_out, ln_g, ln_b,
           *, alpha):
    bsz, seqlen, dm = xp.shape
    ms = xs_in.shape[0]
    mp = bsz * seqlen
    dz = ssd_norm_w.shape[0]
    hs = dz // SSD_HEAD_DIM
    dxbc = ssd_conv_w.shape[1]
    dqkv = gdn_conv_w.shape[1]
    gw = dqkv // 3
    hg = gw // GDN_HEAD_DIM
    assert hs + 2 * hg <= LANES and ms == LANES and seqlen % CHUNK == 0 and mp % ms == 0
    o_dt = dz + dxbc
    o_qkv = o_dt + hs
    o_zg = o_qkv + dqkv
    o_b = o_zg + gw

    w_a = w_in[:, :o_dt].astype(BF16)
    w_b = w_in[:, o_qkv:o_b].astype(BF16)
    w_misc = jnp.concatenate(
        [w_in[:, o_dt:o_qkv], w_in[:, o_b:], jnp.zeros((dm, LANES - hs - 2 * hg), w_in.dtype)], axis=1).astype(BF16)
    proj, misc = _inproj(xp.reshape(mp, dm).astype(BF16), xs_in.reshape(ms, dm).astype(BF16), w_a, w_b, w_misc)
    q_off = dz + dxbc
    zg_off = q_off + dqkv
    row_blk = mp // ms

    dtb_s = _pad_lanes(ssd_dt_bias, 0)
    alog_s = _pad_lanes(ssd_a_log, 0)
    dtb_g = _pad_lanes(gdn_dt_bias, hs + hg)
    alog_g = _pad_lanes(gdn_a_log, hs + hg)
    dvec = jnp.repeat(ssd_d.astype(F32), SSD_HEAD_DIM)[None, :]
    nw_s = ssd_norm_w.astype(F32)[None, :]
    nw_g = gdn_norm_w.astype(F32)[None, :]
    cb_s = ssd_conv_b.astype(F32)[None, :]

    y_ssd, h_p = _ssd_prompt(proj, misc, ssd_conv_w, cb_s, dtb_s, alog_s, dvec, nw_s,
                             bsz=bsz, seqlen=seqlen, dz=dz)
    y_gdn, s_p = _gdn_prompt(proj, misc, gdn_conv_w, dtb_g, alog_g, nw_g, bsz=bsz, seqlen=seqlen,
                             q_off=q_off, zg_off=zg_off, width=gw, b_lane=hs, a_lane=hs + hg)
    w_out_bf = w_out.astype(BF16)
    g2, b2 = ln_g.astype(F32)[None, :], ln_b.astype(F32)[None, :]
    xp_new = _outproj(y_ssd, y_gdn, w_out_bf, xp.reshape(mp, dm), g2, b2, alpha=alpha).reshape(bsz, seqlen, dm)

    xbc_s = _sample_conv(st_ssd_conv.reshape(ms, -1), proj, ssd_conv_w, cb_s,
                         ms=ms, row_blk=row_blk, col_off=dz, width=dxbc)
    qkv_s = _sample_conv(st_gdn_conv.reshape(ms, -1), proj, gdn_conv_w, jnp.zeros((1, dqkv), F32),
                         ms=ms, row_blk=row_blk, col_off=q_off, width=dqkv)
    dt_s, scal = _sample_scalars(misc, dtb_s, alog_s, dtb_g, alog_g, ms=ms, row_blk=row_blk, hs=hs, hg=hg)
    ys_ssd, h_s = _ssd_sample(scal, st_ssd, xbc_s, dt_s, proj, dvec, nw_s, ms=ms, row_blk=row_blk, dz=dz)
    ys_gdn, s_s = _gdn_sample(scal, st_gdn, qkv_s, proj, nw_g, ms=ms, row_blk=row_blk, hs=hs, zg_off=zg_off)
    xs_new = _outproj(ys_ssd, ys_gdn, w_out_bf, xs_in.reshape(ms, dm), g2, b2, alpha=alpha).reshape(ms, 1, dm)

    tail = lambda lo, hi: jnp.stack(
        [proj[(b + 1) * seqlen - (CONV_K - 1):(b + 1) * seqlen, lo:hi] for b in range(bsz)])
    conv_ssd_p = tail(dz, dz + dxbc)
    conv_gdn_p = tail(q_off, q_off + dqkv)
    raw_s = proj[mp:]
    conv_ssd_s = jnp.concatenate([st_ssd_conv[:, 1:], raw_s[:, None, dz:dz + dxbc]], axis=1)
    conv_gdn_s = jnp.concatenate([st_gdn_conv[:, 1:], raw_s[:, None, q_off:q_off + dqkv]], axis=1)
    return (xp_new, xs_new, (h_p, conv_ssd_p, s_p, conv_gdn_p), (h_s, conv_ssd_s, s_s, conv_gdn_s))


def kernel(x_prompt, x_sample, state_ssd, state_ssd_conv, state_gdn, state_gdn_conv, w_in, ssd_conv_w, ssd_conv_b,
           ssd_dt_bias, ssd_a_log, ssd_d, ssd_norm_w, gdn_conv_w, gdn_dt_bias, gdn_a_log, gdn_norm_w, w_out,
           ln_g, ln_b):
    depth = w_in.shape[0]
    alpha = (2 * depth) ** 0.25
    assert x_sample.shape[1] == 1
    hp, hs = x_prompt, x_sample
    p_out, s_out = [], []
    for l in range(depth):
        hp, hs, po, so = _layer(
            hp, hs, state_ssd[l], state_ssd_conv[l], state_gdn[l], state_gdn_conv[l], w_in[l], ssd_conv_w[l],
            ssd_conv_b[l], ssd_dt_bias[l], ssd_a_log[l], ssd_d[l], ssd_norm_w[l], gdn_conv_w[l], gdn_dt_bias[l],
            gdn_a_log[l], gdn_norm_w[l], w_out[l], ln_g[l], ln_b[l], alpha=alpha)
        p_out.append(po)
        s_out.append(so)
    stack = lambda outs, i: jnp.stack([o[i] for o in outs])
    return (hp, hs,
            stack(p_out, 0), stack(p_out, 1), stack(p_out, 2), stack(p_out, 3),
            stack(s_out, 0), stack(s_out, 1), stack(s_out, 2), stack(s_out, 3))
```

```python
import functools

import jax
import jax.numpy as jnp
from jax import lax
from jax.experimental import pallas as pl
from jax.experimental.pallas import tpu as pltpu

F32 = jnp.float32
BF16 = jnp.bfloat16

LANES = 128
SUBLANES = 8
SSD_HEAD_DIM = 64
SSD_GROUPS = 8
SSD_STATE = 128
GDN_HEAD_DIM = 128
CONV_K = 4
NORM_EPS = 1e-6
LN_EPS = 1e-5
CHUNK = 128
NEG_BIG = -1e30
VMEM_LIMIT = 56 * 1024 * 1024


def _divisor_tile(n, target, mult):
    best = None
    t = mult
    while t <= min(n, target):
        if n % t == 0:
            best = t
        t += mult
    assert best is not None, (n, target, mult)
    return best


def _dot(a, b):
    return jnp.dot(a, b, preferred_element_type=F32)


def _dot_nt(a, b):
    return lax.dot_general(a, b, (((1,), (1,)), ((), ())), preferred_element_type=F32)


def _split3(x):
    x1 = x.astype(BF16)
    r1 = x - x1.astype(F32)
    x2 = r1.astype(BF16)
    r2 = r1 - x2.astype(F32)
    return x1, x2, r2.astype(BF16)


def _dot_sel_lhs(sel_bf, x):
    x1, x2, x3 = _split3(x)
    return _dot(sel_bf, x1) + _dot(sel_bf, x2) + _dot(sel_bf, x3)


def _dot_sel_rhs(x, sel_bf):
    x1, x2, x3 = _split3(x)
    return _dot(x1, sel_bf) + _dot(x2, sel_bf) + _dot(x3, sel_bf)


def _sigmoid(x):
    return 1.0 / (1.0 + jnp.exp(-x))


def _silu(x):
    return x * _sigmoid(x)


def _softplus(x):
    return jnp.maximum(x, 0.0) + jnp.log1p(jnp.exp(-jnp.abs(x)))


def _tril01(t, dtype=BF16):
    r = lax.broadcasted_iota(jnp.int32, (t, t), 0)
    c = lax.broadcasted_iota(jnp.int32, (t, t), 1)
    return (r >= c).astype(dtype)


def _expander(rows, cols, per, row_offset=0):
    assert per & (per - 1) == 0
    r = lax.broadcasted_iota(jnp.int32, (rows, cols), 0)
    c = lax.broadcasted_iota(jnp.int32, (rows, cols), 1)
    return (lax.shift_right_logical(c, per.bit_length() - 1) + row_offset == r).astype(BF16)


def _causal_conv_silu(raw_ref, tail_ref, w_ref, bias, t):
    assert CONV_K == 4
    raw = raw_ref[...]
    ext = jnp.concatenate([tail_ref[...], raw], axis=0)
    ext1 = pltpu.roll(ext, 1, 0)
    u = ext1 * w_ref[0:1, :] + ext * w_ref[1:2, :]
    acc = pltpu.roll(u, 2, 0) + (ext1 * w_ref[2:3, :] + ext * w_ref[3:4, :])
    if bias is not None:
        acc = acc + bias
    tail_ref[...] = raw[t - SUBLANES:t, :]
    return _silu(acc[SUBLANES:SUBLANES + t, :])


def _lane_roll(x, shift):
    amount = jnp.where(shift == 0, 0, LANES - shift)
    return pltpu.roll(x, amount, 1)


def _wprep_kernel(a_ref, b0_ref, b1_ref, o_ref, *, na, shift):
    j = pl.program_id(1)
    tn = o_ref.shape[1]

    @pl.when(j < na)
    def _():
        o_ref[...] = a_ref[...].astype(o_ref.dtype)

    @pl.when(j >= na)
    def _():
        cat = jnp.concatenate([b0_ref[...], b1_ref[...]], axis=1)
        if shift:
            cat = pltpu.roll(cat, cat.shape[1] - shift, 1)
        o_ref[...] = cat[:, 0:tn].astype(o_ref.dtype)


def _wprep(w, *, n_a, b_start, n_b):
    k = w.shape[0]
    shift = b_start % LANES
    b_al = b_start - shift
    tk = _divisor_tile(k, 512, 16)
    tn = 1024
    assert n_a % tn == 0 and n_b % tn == 0 and b_al % tn == 0
    na, nb = n_a // tn, n_b // tn
    per = tn // LANES
    return pl.pallas_call(
        functools.partial(_wprep_kernel, na=na, shift=shift),
        grid=(k // tk, na + nb),
        in_specs=[
            pl.BlockSpec((tk, tn), lambda i, j: (i, jnp.minimum(j, na - 1))),
            pl.BlockSpec((tk, tn), lambda i, j: (i, b_al // tn + jnp.maximum(j - na, 0))),
            pl.BlockSpec((tk, LANES), lambda i, j: (i, (b_al // tn + jnp.maximum(j - na, 0) + 1) * per)),
        ],
        out_specs=pl.BlockSpec((tk, tn), lambda i, j: (i, j)),
        out_shape=jax.ShapeDtypeStruct((k, n_a + n_b), BF16),
        compiler_params=pltpu.CompilerParams(dimension_semantics=("parallel", "arbitrary")),
        name="wprep",
    )(w, w, w)


def _inproj_kernel(xp_ref, xs_ref, w_ref, wm_ref, o_ref, om_ref, *, npt, ms):
    i = pl.program_id(0)
    j = pl.program_id(1)

    def emit(x_ref, rows):
        o_ref[0:rows, :] = _dot(x_ref[...], w_ref[...])

        @pl.when(j == 0)
        def _():
            om_ref[0:rows, :] = _dot(x_ref[...], wm_ref[...])

    @pl.when(i < npt)
    def _():
        emit(xp_ref, xp_ref.shape[0])

    @pl.when(i == npt)
    def _():
        emit(xs_ref, ms)


def _inproj(xp_bf, xs_bf, w_main, w_misc):
    mp, k = xp_bf.shape
    ms = xs_bf.shape[0]
    n = w_main.shape[1]
    nm = w_misc.shape[1]
    tm = _divisor_tile(mp, 1024, 16)
    assert ms <= tm
    tn = _divisor_tile(n, 1024, LANES)
    npt = mp // tm
    m = mp + ms
    return pl.pallas_call(
        functools.partial(_inproj_kernel, npt=npt, ms=ms),
        grid=(npt + 1, n // tn),
        in_specs=[
            pl.BlockSpec((tm, k), lambda i, j: (jnp.minimum(i, npt - 1), 0)),
            pl.BlockSpec((ms, k), lambda i, j: (0, 0)),
            pl.BlockSpec((k, tn), lambda i, j: (0, j)),
            pl.BlockSpec((k, nm), lambda i, j: (0, 0)),
        ],
        out_specs=[
            pl.BlockSpec((tm, tn), lambda i, j: (i, j)),
            pl.BlockSpec((tm, nm), lambda i, j: (i, 0)),
        ],
        out_shape=[jax.ShapeDtypeStruct((m, n), F32), jax.ShapeDtypeStruct((m, nm), F32)],
        compiler_params=pltpu.CompilerParams(
            dimension_semantics=("parallel", "arbitrary"), vmem_limit_bytes=VMEM_LIMIT),
        name="inproj",
    )(xp_bf, xs_bf, w_main, w_misc)


SSD_GROUPS_PER_STEP = 8


def _ssd_kernel(z_ref, xs_ref, b_ref, c_ref, misc_ref,
                wx_ref, wb_ref, wc_ref, bx_ref, bb_ref, bc_ref,
                dtb_ref, alog_ref, dvec_ref, nw_ref,
                y_ref, hout_ref,
                ht_ref, tailx_ref, tailb_ref, tailc_ref, *, t, hpg, ngs, nc):
    gblk = pl.program_id(1)
    c = pl.program_id(2)
    wg = hpg * SSD_HEAD_DIM
    n = SSD_STATE
    groups = range(ngs)
    gsl = [slice(g * wg, (g + 1) * wg) for g in groups]
    nsl = [slice(g * n, (g + 1) * n) for g in groups]

    @pl.when(c == 0)
    def _():
        ht_ref[...] = jnp.zeros_like(ht_ref)
        tailx_ref[...] = jnp.zeros_like(tailx_ref)
        tailb_ref[...] = jnp.zeros_like(tailb_ref)
        tailc_ref[...] = jnp.zeros_like(tailc_ref)

    xs = _causal_conv_silu(xs_ref, tailx_ref, wx_ref, bx_ref[...], t)
    bm = _causal_conv_silu(b_ref, tailb_ref, wb_ref, bb_ref[...], t)
    cm = _causal_conv_silu(c_ref, tailc_ref, wc_ref, bc_ref[...], t)

    misc = misc_ref[...]
    dt_all = _softplus(misc + dtb_ref[...])
    da_all = dt_all * (-jnp.exp(alog_ref[...]))
    dt_r = _lane_roll(dt_all, gblk * (ngs * hpg))
    da_r = _lane_roll(da_all, gblk * (ngs * hpg))

    cum = _dot_sel_lhs(_tril01(t), da_r)
    cum_t = cum.T
    expand = _expander(LANES, ngs * wg, SSD_HEAD_DIM)
    cum_b = _dot_sel_rhs(cum, expand)
    dt_b = _dot_sel_rhs(dt_r, expand)
    last_b = cum_b[t - 1:t, :]
    xdt = xs * dt_b
    xw_bf = (xdt * jnp.exp(last_b - cum_b)).astype(BF16)

    bm_bf = [bm[:, nsl[g]].astype(BF16) for g in groups]
    cm_bf = [cm[:, nsl[g]].astype(BF16) for g in groups]
    cb = [_dot_nt(cm_bf[g], bm_bf[g]) for g in groups]
    ht = [ht_ref[g] for g in groups]
    y_inter = [_dot(cm_bf[g], ht[g].astype(BF16)) for g in groups]
    upd = [_dot(bm[:, nsl[g]].T.astype(BF16), xw_bf[:, gsl[g]]) for g in groups]

    rows = lax.broadcasted_iota(jnp.int32, (t, t), 0)
    cols = lax.broadcasted_iota(jnp.int32, (t, t), 1)
    causal = rows >= cols
    lane = lax.broadcasted_iota(jnp.int32, (t, LANES), 1)
    y_intra = []
    for g in groups:
        for p in range(hpg // 2):
            wms = []
            for h in (g * hpg + 2 * p, g * hpg + 2 * p + 1):
                col = jnp.broadcast_to(cum[:, h:h + 1], (t, t))
                row = jnp.broadcast_to(cum_t[h:h + 1, :], (t, t))
                dec = jnp.exp(jnp.where(causal, col - row, NEG_BIG))
                wms.append((cb[g] * dec).astype(BF16))
            xp = xdt[:, g * wg + p * LANES:g * wg + (p + 1) * LANES]
            rhs = jnp.concatenate([jnp.where(lane < SSD_HEAD_DIM, xp, 0.0),
                                   jnp.where(lane >= SSD_HEAD_DIM, xp, 0.0)], axis=0).astype(BF16)
            y_intra.append(_dot(jnp.concatenate(wms, axis=1), rhs))
    y = jnp.concatenate(y_intra, axis=1) if len(y_intra) > 1 else y_intra[0]
    y_int = jnp.concatenate(y_inter, axis=1) if ngs > 1 else y_inter[0]
    y = y + y_int * jnp.exp(cum_b) + xs * dvec_ref[...]
    u = y * _silu(z_ref[...])
    outs = []
    for g in groups:
        ug = u[:, gsl[g]]
        ms = jnp.mean(ug * ug, axis=-1, keepdims=True)
        outs.append(ug * lax.rsqrt(ms + NORM_EPS))
    un = jnp.concatenate(outs, axis=1) if ngs > 1 else outs[0]
    y_ref[...] = (un * nw_ref[...]).astype(y_ref.dtype)

    elast = jnp.exp(last_b)
    for g in groups:
        ht_ref[g] = ht[g] * elast[:, gsl[g]] + upd[g]

    @pl.when(c == nc - 1)
    def _():
        for g in groups:
            hout_ref[0, g * hpg:(g + 1) * hpg] = ht_ref[g].T.reshape(hpg, SSD_HEAD_DIM, SSD_STATE)


def _ssd_prompt(proj, misc, conv_w, conv_b, dtb_pad, alog_pad, dvec, norm_w, *, bsz, seqlen, dz):
    t = CHUNK
    nc = seqlen // t
    hs = dz // SSD_HEAD_DIM
    hpg = hs // SSD_GROUPS
    wg = hpg * SSD_HEAD_DIM
    n = SSD_STATE
    ngs = SSD_GROUPS_PER_STEP
    assert SSD_GROUPS % ngs == 0 and hpg % 2 == 0
    ngb = SSD_GROUPS // ngs
    wv, wn = ngs * wg, ngs * n
    xs_blk = dz // wv
    b_blk = 2 * dz // wn
    c_blk = b_blk + ngb
    cwb = dz // wn
    row = lambda b, g, c: b * nc + c
    kern = functools.partial(_ssd_kernel, t=t, hpg=hpg, ngs=ngs, nc=nc)
    return pl.pallas_call(
        kern,
        grid=(bsz, ngb, nc),
        in_specs=[
            pl.BlockSpec((t, wv), lambda b, g, c: (row(b, g, c), g)),
            pl.BlockSpec((t, wv), lambda b, g, c: (row(b, g, c), xs_blk + g)),
            pl.BlockSpec((t, wn), lambda b, g, c: (row(b, g, c), b_blk + g)),
            pl.BlockSpec((t, wn), lambda b, g, c: (row(b, g, c), c_blk + g)),
            pl.BlockSpec((t, LANES), lambda b, g, c: (row(b, g, c), 0)),
            pl.BlockSpec((CONV_K, wv), lambda b, g, c: (0, g)),
            pl.BlockSpec((CONV_K, wn), lambda b, g, c: (0, cwb + g)),
            pl.BlockSpec((CONV_K, wn), lambda b, g, c: (0, cwb + ngb + g)),
            pl.BlockSpec((1, wv), lambda b, g, c: (0, g)),
            pl.BlockSpec((1, wn), lambda b, g, c: (0, cwb + g)),
            pl.BlockSpec((1, wn), lambda b, g, c: (0, cwb + ngb + g)),
            pl.BlockSpec((1, LANES), lambda b, g, c: (0, 0)),
            pl.BlockSpec((1, LANES), lambda b, g, c: (0, 0)),
            pl.BlockSpec((1, wv), lambda b, g, c: (0, g)),
            pl.BlockSpec((1, wv), lambda b, g, c: (0, g)),
        ],
        out_specs=[
            pl.BlockSpec((t, wv), lambda b, g, c: (row(b, g, c), g)),
            pl.BlockSpec((1, ngs * hpg, SSD_HEAD_DIM, SSD_STATE), lambda b, g, c: (b, g, 0, 0)),
        ],
        out_shape=[
            jax.ShapeDtypeStruct((bsz * seqlen, dz), BF16),
            jax.ShapeDtypeStruct((bsz, hs, SSD_HEAD_DIM, SSD_STATE), F32),
        ],
        scratch_shapes=[
            pltpu.VMEM((ngs, n, wg), F32),
            pltpu.VMEM((SUBLANES, wv), F32),
            pltpu.VMEM((SUBLANES, wn), F32),
            pltpu.VMEM((SUBLANES, wn), F32),
        ],
        compiler_params=pltpu.CompilerParams(
            dimension_semantics=("parallel", "parallel", "arbitrary"), vmem_limit_bytes=VMEM_LIMIT),
        name="ssd_prompt",
    )(proj, proj, proj, proj, misc, conv_w, conv_w, conv_w, conv_b, conv_b, conv_b,
      dtb_pad, alog_pad, dvec, norm_w)


INV_BASE = 16
GDN_HEADS_PER_STEP = 8


def _inverse_masks(t):
    rows = lax.broadcasted_iota(jnp.int32, (t, t), 0)
    cols = lax.broadcasted_iota(jnp.int32, (t, t), 1)
    eye = jnp.where(rows == cols, 1.0, 0.0)
    shift = INV_BASE.bit_length() - 1
    same = lax.shift_right_logical(rows, shift) == lax.shift_right_logical(cols, shift)
    masks = [same]
    b = INV_BASE
    while b < t:
        shift += 1
        same2 = lax.shift_right_logical(rows, shift) == lax.shift_right_logical(cols, shift)
        masks.append(jnp.logical_and(same2, jnp.logical_not(same)))
        same = same2
        b *= 2
    return eye, masks


def _unit_lower_inverse(ns, eye, masks):
    heads = range(len(ns))
    nks = [jnp.where(masks[0], n, 0.0) for n in ns]
    qs = [-nk for nk in nks]
    k = 2
    while k < INV_BASE:
        nk_bfs = [nk.astype(BF16) for nk in nks]
        nks = [_dot(nk_bf, nk_bf) for nk_bf in nk_bfs]
        upd = [_dot(qs[i].astype(BF16), nks[i].astype(BF16)) for i in heads]
        qs = [qs[i] + nks[i] + upd[i] for i in heads]
        k *= 2
    ds = [eye + q for q in qs]
    for off in masks[1:]:
        d_bfs = [d.astype(BF16) for d in ds]
        dc = [_dot(d_bfs[i], jnp.where(off, ns[i], 0.0).astype(BF16)) for i in heads]
        dcd = [_dot(dc[i].astype(BF16), d_bfs[i]) for i in heads]
        ds = [ds[i] - dcd[i] for i in heads]
    return ds


def _gdn_kernel(q_ref, k_ref, v_ref, zg_ref, misc_ref,
                wq_ref, wk_ref, wv_ref, dtb_ref, alog_ref, nw_ref,
                y_ref, sout_ref,
                s_ref, tailq_ref, tailk_ref, tailv_ref, *, t, hb, nc, b_lane, a_lane):
    hblk = pl.program_id(1)
    c = pl.program_id(2)
    d = GDN_HEAD_DIM

    @pl.when(c == 0)
    def _():
        s_ref[...] = jnp.zeros_like(s_ref)
        for tail in (tailq_ref, tailk_ref, tailv_ref):
            tail[...] = jnp.zeros_like(tail)

    q = _causal_conv_silu(q_ref, tailq_ref, wq_ref, None, t)
    k = _causal_conv_silu(k_ref, tailk_ref, wk_ref, None, t)
    v = _causal_conv_silu(v_ref, tailv_ref, wv_ref, None, t)
    zg = zg_ref[...]

    misc = misc_ref[...]
    beta_r = _lane_roll(_sigmoid(misc), b_lane + hblk * hb)
    g_all = -jnp.exp(alog_ref[...]) * _softplus(misc + dtb_ref[...])
    g_r = _lane_roll(g_all, a_lane + hblk * hb)

    tril = _tril01(t)
    cum = _dot_sel_lhs(tril, g_r)
    cum_t = cum.T
    ecum = jnp.exp(cum)
    last = cum[t - 1:t, :]
    wdec = jnp.exp(last - cum)
    elast = jnp.exp(last)

    rows = lax.broadcasted_iota(jnp.int32, (t, t), 0)
    cols = lax.broadcasted_iota(jnp.int32, (t, t), 1)
    causal = rows >= cols
    strict = rows > cols
    eye, inv_masks = _inverse_masks(t)

    heads = range(hb)
    sls = [slice(i * d, (i + 1) * d) for i in heads]
    colb = lambda a, i, w: jnp.broadcast_to(a[:, i:i + 1], (t, w))
    qn = [q[:, sl] * lax.rsqrt(jnp.sum(q[:, sl] * q[:, sl], axis=-1, keepdims=True) + NORM_EPS) * (d ** -0.5)
          for sl in sls]
    kn = [k[:, sl] * lax.rsqrt(jnp.sum(k[:, sl] * k[:, sl], axis=-1, keepdims=True) + NORM_EPS) for sl in sls]
    dec = [jnp.exp(jnp.where(causal, colb(cum, i, t) - jnp.broadcast_to(cum_t[i:i + 1, :], (t, t)), NEG_BIG))
           for i in heads]
    kn_bf = [x.astype(BF16) for x in kn]
    qn_bf = [x.astype(BF16) for x in qn]
    kk = [_dot_nt(kn_bf[i], kn_bf[i]) for i in heads]
    qk = [_dot_nt(qn_bf[i], kn_bf[i]) for i in heads]
    ns = [jnp.where(strict, kk[i] * dec[i] * colb(beta_r, i, t), 0.0) for i in heads]
    minv = _unit_lower_inverse(ns, eye, inv_masks)
    rhs = [jnp.concatenate([v[:, sls[i]] * colb(beta_r, i, d), kn[i] * colb(beta_r, i, d) * colb(ecum, i, d)],
                           axis=1).astype(BF16) for i in heads]
    x = [_dot(minv[i].astype(BF16), rhs[i]) for i in heads]
    s_old = [s_ref[i] for i in heads]
    r = [_dot(jnp.concatenate([x[i][:, d:], qn[i] * colb(ecum, i, d)], axis=0).astype(BF16),
              s_old[i].astype(BF16)) for i in heads]
    v_new_bf = [(x[i][:, :d] - r[i][:t]).astype(BF16) for i in heads]
    o = [_dot((qk[i] * dec[i]).astype(BF16), v_new_bf[i]) for i in heads]
    upd = [_dot((kn[i] * colb(wdec, i, d)).T.astype(BF16), v_new_bf[i]) for i in heads]
    outs = []
    for i in heads:
        s_ref[i] = s_old[i] * elast[:, i:i + 1] + upd[i]
        oi = r[i][t:] + o[i]
        ms = jnp.mean(oi * oi, axis=-1, keepdims=True)
        outs.append(oi * lax.rsqrt(ms + NORM_EPS) * nw_ref[...] * _silu(zg[:, sls[i]]))
    y = jnp.concatenate(outs, axis=1) if hb > 1 else outs[0]
    y_ref[...] = y.astype(y_ref.dtype)

    @pl.when(c == nc - 1)
    def _():
        sout_ref[0] = s_ref[...]


def _gdn_prompt(proj, misc, conv_w, dtb_pad, alog_pad, norm_w, *, bsz, seqlen, q_off, zg_off, width,
                b_lane, a_lane):
    t = CHUNK
    nc = seqlen // t
    d = GDN_HEAD_DIM
    hg = width // d
    hb = _divisor_tile(hg, GDN_HEADS_PER_STEP, 1)
    bw = hb * d
    nhb = hg // hb
    qb, kb, vb, zb = q_off // bw, (q_off + width) // bw, (q_off + 2 * width) // bw, zg_off // bw
    wkb, wvb = width // bw, 2 * width // bw
    row = lambda b, h, c: b * nc + c
    kern = functools.partial(_gdn_kernel, t=t, hb=hb, nc=nc, b_lane=b_lane, a_lane=a_lane)
    return pl.pallas_call(
        kern,
        grid=(bsz, nhb, nc),
        in_specs=[
            pl.BlockSpec((t, bw), lambda b, h, c: (row(b, h, c), qb + h)),
            pl.BlockSpec((t, bw), lambda b, h, c: (row(b, h, c), kb + h)),
            pl.BlockSpec((t, bw), lambda b, h, c: (row(b, h, c), vb + h)),
            pl.BlockSpec((t, bw), lambda b, h, c: (row(b, h, c), zb + h)),
            pl.BlockSpec((t, LANES), lambda b, h, c: (row(b, h, c), 0)),
            pl.BlockSpec((CONV_K, bw), lambda b, h, c: (0, h)),
            pl.BlockSpec((CONV_K, bw), lambda b, h, c: (0, wkb + h)),
            pl.BlockSpec((CONV_K, bw), lambda b, h, c: (0, wvb + h)),
            pl.BlockSpec((1, LANES), lambda b, h, c: (0, 0)),
            pl.BlockSpec((1, LANES), lambda b, h, c: (0, 0)),
            pl.BlockSpec((1, d), lambda b, h, c: (0, 0)),
        ],
        out_specs=[
            pl.BlockSpec((t, bw), lambda b, h, c: (row(b, h, c), h)),
            pl.BlockSpec((1, hb, d, d), lambda b, h, c: (b, h, 0, 0)),
        ],
        out_shape=[
            jax.ShapeDtypeStruct((bsz * seqlen, width), BF16),
            jax.ShapeDtypeStruct((bsz, hg, d, d), F32),
        ],
        scratch_shapes=[
            pltpu.VMEM((hb, d, d), F32),
            pltpu.VMEM((SUBLANES, bw), F32),
            pltpu.VMEM((SUBLANES, bw), F32),
            pltpu.VMEM((SUBLANES, bw), F32),
        ],
        compiler_params=pltpu.CompilerParams(
            dimension_semantics=("parallel", "parallel", "arbitrary"), vmem_limit_bytes=VMEM_LIMIT),
        name="gdn_prompt",
    )(proj, proj, proj, proj, misc, conv_w, conv_w, conv_w, dtb_pad, alog_pad, norm_w)


def _sample_conv_kernel(s0_ref, s1_ref, s2_ref, u_ref, w_ref, b_ref, o_ref):
    acc = b_ref[...] + s0_ref[...] * w_ref[0:1, :]
    acc = acc + s1_ref[...] * w_ref[1:2, :]
    acc = acc + s2_ref[...] * w_ref[2:3, :]
    acc = acc + u_ref[...] * w_ref[3:4, :]
    o_ref[...] = _silu(acc)


def _sample_conv(state2d, proj, conv_w, conv_b, *, ms, row_blk, col_off, width):
    wb = _divisor_tile(width, 512, LANES)
    nb = width // wb
    cb = col_off // wb
    return pl.pallas_call(
        _sample_conv_kernel,
        grid=(nb,),
        in_specs=[
            pl.BlockSpec((ms, wb), lambda j: (0, j)),
            pl.BlockSpec((ms, wb), lambda j: (0, nb + j)),
            pl.BlockSpec((ms, wb), lambda j: (0, 2 * nb + j)),
            pl.BlockSpec((ms, wb), lambda j: (row_blk, cb + j)),
            pl.BlockSpec((CONV_K, wb), lambda j: (0, j)),
            pl.BlockSpec((1, wb), lambda j: (0, j)),
        ],
        out_specs=pl.BlockSpec((ms, wb), lambda j: (0, j)),
        out_shape=jax.ShapeDtypeStruct((ms, width), F32),
        compiler_params=pltpu.CompilerParams(dimension_semantics=("parallel",)),
        name="sample_conv",
    )(state2d, state2d, state2d, proj, conv_w, conv_b)


def _sample_scalars_kernel(misc_ref, dtb_ref, alog_s_ref, dtb_g_ref, alog_g_ref, dt_ref, sc_ref, *, hs, hg):
    misc = misc_ref[...]
    lane = lax.broadcasted_iota(jnp.int32, misc.shape, 1)
    dt = _softplus(misc + dtb_ref[...])
    eda = jnp.exp(dt * (-jnp.exp(alog_s_ref[...])))
    beta = _sigmoid(misc)
    eg = jnp.exp(-jnp.exp(alog_g_ref[...]) * _softplus(misc + dtb_g_ref[...]))
    dt_ref[...] = dt
    sc_ref[...] = jnp.where(lane < hs, eda, jnp.where(lane < hs + hg, beta, eg))


def _sample_scalars(misc, dtb_s, alog_s, dtb_g, alog_g, *, ms, row_blk, hs, hg):
    vec = pl.BlockSpec((1, LANES), lambda i: (0, 0))
    return pl.pallas_call(
        functools.partial(_sample_scalars_kernel, hs=hs, hg=hg),
        grid=(1,),
        in_specs=[pl.BlockSpec((ms, LANES), lambda i: (row_blk, 0)), vec, vec, vec, vec],
        out_specs=[pl.BlockSpec((ms, LANES), lambda i: (0, 0)), pl.BlockSpec((ms, LANES), lambda i: (0, 0))],
        out_shape=[jax.ShapeDtypeStruct((ms, LANES), F32), jax.ShapeDtypeStruct((ms, LANES), F32)],
        name="sample_scalars",
    )(misc, dtb_s, alog_s, dtb_g, alog_g)


def _transpose_blocks(x, nblk):
    return jnp.concatenate([x[:, j * LANES:(j + 1) * LANES].T for j in range(nblk)], axis=0)


def _row_group(b):
    b8 = pl.multiple_of(lax.shift_right_logical(b, 3) * SUBLANES, SUBLANES)
    in_group = lax.broadcasted_iota(jnp.int32, (SUBLANES, LANES), 0) == jnp.bitwise_and(b, SUBLANES - 1)
    return b8, in_group


def _untranspose_blocks(x, nblk):
    return jnp.concatenate([x[j * LANES:(j + 1) * LANES, :].T for j in range(nblk)], axis=1)


def _ssd_sample_kernel(sc_smem, h_ref, xbc_ref, dt_ref, z_ref, dvec_ref, nw_ref,
                       y_ref, hout_ref,
                       xt_ref, yt_ref, *, ms, hs, dz):
    b = pl.program_id(0)
    hpg = hs // SSD_GROUPS
    wg = hpg * SSD_HEAD_DIM

    @pl.when(b == 0)
    def _():
        xs = xbc_ref[:, 0:dz]
        dt_b = _dot_sel_rhs(dt_ref[...], _expander(LANES, dz, SSD_HEAD_DIM))
        xt_ref[...] = _transpose_blocks(xs * dt_b, dz // LANES)
        yt_ref[...] = jnp.zeros_like(yt_ref)

    seq_rows = lax.broadcasted_iota(jnp.int32, (ms, SSD_STATE), 0)
    seq_lanes = lax.broadcasted_iota(jnp.int32, (wg, LANES), 1)
    b8, in_group = _row_group(b)
    for g in range(SSD_GROUPS):
        hgrp = h_ref[0, g * hpg:(g + 1) * hpg].reshape(wg, SSD_STATE)
        bm = xbc_ref[:, dz + g * SSD_STATE: dz + (g + 1) * SSD_STATE]
        bm = jnp.where(seq_rows == b, bm, 0.0).astype(BF16)
        upd = _dot(xt_ref[g * wg:(g + 1) * wg, :].astype(BF16), bm)
        decay = jnp.concatenate(
            [jnp.full((SSD_HEAD_DIM, SSD_STATE), sc_smem[b, g * hpg + h], F32) for h in range(hpg)], axis=0)
        hn = hgrp * decay + upd
        hout_ref[0, g * hpg:(g + 1) * hpg] = hn.reshape(hpg, SSD_HEAD_DIM, SSD_STATE)
        c_off = dz + SSD_GROUPS * SSD_STATE + g * SSD_STATE
        crow = jnp.sum(jnp.where(in_group, xbc_ref[pl.ds(b8, SUBLANES), c_off:c_off + SSD_STATE], 0.0),
                       axis=0, keepdims=True)
        ycol = jnp.sum(hn * crow, axis=1, keepdims=True)
        cur = yt_ref[g * wg:(g + 1) * wg, :]
        yt_ref[g * wg:(g + 1) * wg, :] = jnp.where(seq_lanes == b, ycol, cur)

    @pl.when(b == ms - 1)
    def _():
        y = _untranspose_blocks(yt_ref[...], dz // LANES)
        y = y + xbc_ref[:, 0:dz] * dvec_ref[...]
        u = y * _silu(z_ref[...])
        outs = []
        for g in range(SSD_GROUPS):
            ug = u[:, g * wg:(g + 1) * wg]
            msq = jnp.mean(ug * ug, axis=-1, keepdims=True)
            outs.append(ug * lax.rsqrt(msq + NORM_EPS))
        y_ref[...] = (jnp.concatenate(outs, axis=1) * nw_ref[...]).astype(y_ref.dtype)


def _ssd_sample(scal, h_state, xbc_s, dt_s, proj, dvec, norm_w, *, ms, row_blk, dz):
    hs = dz // SSD_HEAD_DIM
    cw = xbc_s.shape[1]
    full = lambda shape: pl.BlockSpec(shape, lambda b: tuple(0 for _ in shape))
    return pl.pallas_call(
        functools.partial(_ssd_sample_kernel, ms=ms, hs=hs, dz=dz),
        grid=(ms,),
        in_specs=[
            pl.BlockSpec(memory_space=pltpu.SMEM),
            pl.BlockSpec((1, hs, SSD_HEAD_DIM, SSD_STATE), lambda b: (b, 0, 0, 0)),
            full((ms, cw)),
            full((ms, LANES)),
            pl.BlockSpec((ms, dz), lambda b: (row_blk, 0)),
            full((1, dz)),
            full((1, dz)),
        ],
        out_specs=[
            full((ms, dz)),
            pl.BlockSpec((1, hs, SSD_HEAD_DIM, SSD_STATE), lambda b: (b, 0, 0, 0)),
        ],
        out_shape=[
            jax.ShapeDtypeStruct((ms, dz), BF16),
            jax.ShapeDtypeStruct(h_state.shape, F32),
        ],
        scratch_shapes=[pltpu.VMEM((dz, LANES), F32), pltpu.VMEM((dz, LANES), F32)],
        compiler_params=pltpu.CompilerParams(
            dimension_semantics=("arbitrary",), vmem_limit_bytes=VMEM_LIMIT),
        name="ssd_sample",
    )(scal, h_state, xbc_s, dt_s, proj, dvec, norm_w)


def _gdn_sample_kernel(sc_smem, s_ref, qkv_ref, sc_ref, zg0_ref, zg1_ref, nw_ref,
                       y_ref, sout_ref,
                       w_ref, qe_ref, vb_ref, qk_ref, kt_ref, o_ref, *, ms, hs, hg):
    b = pl.program_id(0)
    d = GDN_HEAD_DIM
    width = hg * d

    @pl.when(b == 0)
    def _():
        sc = sc_ref[...]
        beta_b = _dot_sel_rhs(sc, _expander(LANES, width, d, hs))
        eg_b = _dot_sel_rhs(sc, _expander(LANES, width, d, hs + hg))
        for h in range(hg):
            sl = slice(h * d, (h + 1) * d)
            qh = qkv_ref[:, h * d:(h + 1) * d]
            kh = qkv_ref[:, width + h * d: width + (h + 1) * d]
            vh = qkv_ref[:, 2 * width + h * d: 2 * width + (h + 1) * d]
            qn = qh * lax.rsqrt(jnp.sum(qh * qh, axis=-1, keepdims=True) + NORM_EPS) * (d ** -0.5)
            kn = kh * lax.rsqrt(jnp.sum(kh * kh, axis=-1, keepdims=True) + NORM_EPS)
            w_ref[:, sl] = kn * beta_b[:, sl] * eg_b[:, sl]
            qe_ref[:, sl] = qn * eg_b[:, sl]
            vb_ref[:, sl] = vh * beta_b[:, sl]
            qk_ref[:, sl] = jnp.broadcast_to(jnp.sum(qn * kn, axis=-1, keepdims=True), (ms, d))
            kt_ref[sl, :] = kn.T

    seq_rows = lax.broadcasted_iota(jnp.int32, (ms, d), 0)
    b8, in_group = _row_group(b)
    for h in range(hg):
        sl = slice(h * d, (h + 1) * d)
        s = s_ref[0, h]
        w8 = jnp.where(in_group, w_ref[pl.ds(b8, SUBLANES), sl], 0.0)
        q8 = jnp.where(in_group, qe_ref[pl.ds(b8, SUBLANES), sl], 0.0)
        r = _dot(jnp.concatenate([w8, q8], axis=0).astype(BF16), s.astype(BF16))
        vb8 = jnp.where(in_group, vb_ref[pl.ds(b8, SUBLANES), sl], 0.0)
        v_new8 = vb8 - r[0:SUBLANES]
        v_new = jnp.sum(v_new8, axis=0, keepdims=True)
        o8 = r[SUBLANES:] + qk_ref[pl.ds(b8, SUBLANES), sl] * v_new8
        o_ref[pl.ds(b8, SUBLANES), sl] = jnp.where(in_group, o8, o_ref[pl.ds(b8, SUBLANES), sl])
        vm = jnp.where(seq_rows == b, jnp.broadcast_to(v_new, (ms, d)), 0.0).astype(BF16)
        s_new = s * sc_smem[b, hs + hg + h] + _dot(kt_ref[sl, :].astype(BF16), vm)
        sout_ref[0, h] = s_new

    @pl.when(b == ms - 1)
    def _():
        outs = []
        for h in range(hg):
            sl = slice(h * d, (h + 1) * d)
            o = o_ref[:, sl]
            msq = jnp.mean(o * o, axis=-1, keepdims=True)
            zg_ref, zoff = (zg0_ref, 0) if h < hg // 2 else (zg1_ref, width // 2)
            zg = zg_ref[:, h * d - zoff:(h + 1) * d - zoff]
            outs.append(o * lax.rsqrt(msq + NORM_EPS) * nw_ref[...] * _silu(zg))
        y_ref[...] = jnp.concatenate(outs, axis=1).astype(y_ref.dtype)


def _gdn_sample(scal, s_state, qkv_s, proj, norm_w, *, ms, row_blk, hs, zg_off):
    d = GDN_HEAD_DIM
    hg = s_state.shape[1]
    width = hg * d
    full = lambda shape: pl.BlockSpec(shape, lambda b: tuple(0 for _ in shape))
    act = pltpu.VMEM((ms, width), F32)
    return pl.pallas_call(
        functools.partial(_gdn_sample_kernel, ms=ms, hs=hs, hg=hg),
        grid=(ms,),
        in_specs=[
            pl.BlockSpec(memory_space=pltpu.SMEM),
            pl.BlockSpec((1, hg, d, d), lambda b: (b, 0, 0, 0)),
            full((ms, 3 * width)),
            full((ms, LANES)),
            pl.BlockSpec((ms, width // 2), lambda b: (row_blk, zg_off // (width // 2))),
            pl.BlockSpec((ms, width // 2), lambda b: (row_blk, zg_off // (width // 2) + 1)),
            full((1, d)),
        ],
        out_specs=[
            full((ms, width)),
            pl.BlockSpec((1, hg, d, d), lambda b: (b, 0, 0, 0)),
        ],
        out_shape=[
            jax.ShapeDtypeStruct((ms, width), BF16),
            jax.ShapeDtypeStruct(s_state.shape, F32),
        ],
        scratch_shapes=[act, act, act, act, pltpu.VMEM((width, LANES), F32), act],
        compiler_params=pltpu.CompilerParams(
            dimension_semantics=("arbitrary",), vmem_limit_bytes=VMEM_LIMIT),
        name="gdn_sample",
    )(scal, s_state, qkv_s, scal, proj, proj, norm_w)


def _outproj_kernel(ya_ref, yb_ref, w_ref, x_ref, g_ref, b_ref, o_ref, *, nkh, nk, alpha):
    k = pl.program_id(1)

    @pl.when(k == 0)
    def _():
        o_ref[...] = jnp.zeros_like(o_ref)

    @pl.when(k < nkh)
    def _():
        o_ref[...] += _dot(ya_ref[...], w_ref[...])

    @pl.when(k >= nkh)
    def _():
        o_ref[...] += _dot(yb_ref[...], w_ref[...])

    @pl.when(k == nk - 1)
    def _():
        y = alpha * x_ref[...] + o_ref[...]
        mu = jnp.mean(y, axis=-1, keepdims=True)
        yc = y - mu
        var = jnp.mean(yc * yc, axis=-1, keepdims=True)
        o_ref[...] = yc * lax.rsqrt(var + LN_EPS) * g_ref[...] + b_ref[...]


def _outproj(ya, yb, w_bf, x, ln_g, ln_b, *, alpha):
    m, ka = ya.shape
    kb = yb.shape[1]
    d = w_bf.shape[1]
    tm = _divisor_tile(m, 512, SUBLANES)
    tk = _divisor_tile(min(ka, kb), 512, LANES)
    assert ka % tk == 0 and kb % tk == 0
    nkh = ka // tk
    nk = nkh + kb // tk
    return pl.pallas_call(
        functools.partial(_outproj_kernel, nkh=nkh, nk=nk, alpha=alpha),
        grid=(m // tm, nk),
        in_specs=[
            pl.BlockSpec((tm, tk), lambda i, k: (i, jnp.minimum(k, nkh - 1))),
            pl.BlockSpec((tm, tk), lambda i, k: (i, jnp.maximum(k - nkh, 0))),
            pl.BlockSpec((tk, d), lambda i, k: (k, 0)),
            pl.BlockSpec((tm, d), lambda i, k: (i, 0)),
            pl.BlockSpec((1, d), lambda i, k: (0, 0)),
            pl.BlockSpec((1, d), lambda i, k: (0, 0)),
        ],
        out_specs=pl.BlockSpec((tm, d), lambda i, k: (i, 0)),
        out_shape=jax.ShapeDtypeStruct((m, d), F32),
        compiler_params=pltpu.CompilerParams(
            dimension_semantics=("parallel", "arbitrary"), vmem_limit_bytes=VMEM_LIMIT),
        name="outproj",
    )(ya, yb, w_bf, x, ln_g, ln_b)


def _pad_lanes(vec, offset):
    out = jnp.zeros((1, LANES), F32)
    return lax.dynamic_update_slice(out, vec.astype(F32)[None, :], (0, offset))


def _layer(xp, xs_in, st_ssd, st_ssd_conv, st_gdn, st_gdn_conv, w_in, ssd_conv_w, ssd_conv_b, ssd_dt_bias,
           ssd_a_log, ssd_d, ssd_norm_w, gdn_conv_w, gdn_dt_bias, gdn_a_log, gdn_norm_w, w_out, ln_g, ln_b,
           *, alpha):
    bsz, seqlen, dm = xp.shape
    ms = xs_in.shape[0]
    mp = bsz * seqlen
    dz = ssd_norm_w.shape[0]
    hs = dz // SSD_HEAD_DIM
    dxbc = ssd_conv_w.shape[1]
    dqkv = gdn_conv_w.shape[1]
    gw = dqkv // 3
    hg = gw // GDN_HEAD_DIM
    assert hs + 2 * hg <= LANES and ms == LANES and seqlen % CHUNK == 0 and mp % ms == 0
    o_dt = dz + dxbc
    o_qkv = o_dt + hs
    o_zg = o_qkv + dqkv
    o_b = o_zg + gw

    w_main = _wprep(w_in, n_a=o_dt, b_start=o_qkv, n_b=o_b - o_qkv)
    w_misc = jnp.concatenate(
        [w_in[:, o_dt:o_qkv], w_in[:, o_b:], jnp.zeros((dm, LANES - hs - 2 * hg), w_in.dtype)], axis=1).astype(BF16)
    proj, misc = _inproj(xp.reshape(mp, dm).astype(BF16), xs_in.reshape(ms, dm).astype(BF16), w_main, w_misc)
    q_off = dz + dxbc
    zg_off = q_off + dqkv
    row_blk = mp // ms

    dtb_s = _pad_lanes(ssd_dt_bias, 0)
    alog_s = _pad_lanes(ssd_a_log, 0)
    dtb_g = _pad_lanes(gdn_dt_bias, hs + hg)
    alog_g = _pad_lanes(gdn_a_log, hs + hg)
    dvec = jnp.repeat(ssd_d.astype(F32), SSD_HEAD_DIM)[None, :]
    nw_s = ssd_norm_w.astype(F32)[None, :]
    nw_g = gdn_norm_w.astype(F32)[None, :]
    cb_s = ssd_conv_b.astype(F32)[None, :]

    y_ssd, h_p = _ssd_prompt(proj, misc, ssd_conv_w, cb_s, dtb_s, alog_s, dvec, nw_s,
                             bsz=bsz, seqlen=seqlen, dz=dz)
    y_gdn, s_p = _gdn_prompt(proj, misc, gdn_conv_w, dtb_g, alog_g, nw_g, bsz=bsz, seqlen=seqlen,
                             q_off=q_off, zg_off=zg_off, width=gw, b_lane=hs, a_lane=hs + hg)
    w_out_bf = w_out.astype(BF16)
    g2, b2 = ln_g.astype(F32)[None, :], ln_b.astype(F32)[None, :]
    xp_new = _outproj(y_ssd, y_gdn, w_out_bf, xp.reshape(mp, dm), g2, b2, alpha=alpha).reshape(bsz, seqlen, dm)

    xbc_s = _sample_conv(st_ssd_conv.reshape(ms, -1), proj, ssd_conv_w, cb_s,
                         ms=ms, row_blk=row_blk, col_off=dz, width=dxbc)
    qkv_s = _sample_conv(st_gdn_conv.reshape(ms, -1), proj, gdn_conv_w, jnp.zeros((1, dqkv), F32),
                         ms=ms, row_blk=row_blk, col_off=q_off, width=dqkv)
    dt_s, scal = _sample_scalars(misc, dtb_s, alog_s, dtb_g, alog_g, ms=ms, row_blk=row_blk, hs=hs, hg=hg)
    ys_ssd, h_s = _ssd_sample(scal, st_ssd, xbc_s, dt_s, proj, dvec, nw_s, ms=ms, row_blk=row_blk, dz=dz)
    ys_gdn, s_s = _gdn_sample(scal, st_gdn, qkv_s, proj, nw_g, ms=ms, row_blk=row_blk, hs=hs, zg_off=zg_off)
    xs_new = _outproj(ys_ssd, ys_gdn, w_out_bf, xs_in.reshape(ms, dm), g2, b2, alpha=alpha).reshape(ms, 1, dm)

    tail = lambda lo, hi: jnp.stack(
        [proj[(b + 1) * seqlen - (CONV_K - 1):(b + 1) * seqlen, lo:hi] for b in range(bsz)])
    conv_ssd_p = tail(dz, dz + dxbc)
    conv_gdn_p = tail(q_off, q_off + dqkv)
    raw_s = proj[mp:]
    conv_ssd_s = jnp.concatenate([st_ssd_conv[:, 1:], raw_s[:, None, dz:dz + dxbc]], axis=1)
    conv_gdn_s = jnp.concatenate([st_gdn_conv[:, 1:], raw_s[:, None, q_off:q_off + dqkv]], axis=1)
    return (xp_new, xs_new, (h_p, conv_ssd_p, s_p, conv_gdn_p), (h_s, conv_ssd_s, s_s, conv_gdn_s))


def kernel(x_prompt, x_sample, state_ssd, state_ssd_conv, state_gdn, state_gdn_conv, w_in, ssd_conv_w, ssd_conv_b,
           ssd_dt_bias, ssd_a_log, ssd_d, ssd_norm_w, gdn_conv_w, gdn_dt_bias, gdn_a_log, gdn_norm_w, w_out,
           ln_g, ln_b):
    depth = w_in.shape[0]
    alpha = (2 * depth) ** 0.25
    assert x_sample.shape[1] == 1
    hp, hs = x_prompt, x_sample
    p_out, s_out = [], []
    for l in range(depth):
        hp, hs, po, so = _layer(
            hp, hs, state_ssd[l], state_ssd_conv[l], state_gdn[l], state_gdn_conv[l], w_in[l], ssd_conv_w[l],
            ssd_conv_b[l], ssd_dt_bias[l], ssd_a_log[l], ssd_d[l], ssd_norm_w[l], gdn_conv_w[l], gdn_dt_bias[l],
            gdn_a_log[l], gdn_norm_w[l], w_out[l], ln_g[l], ln_b[l], alpha=alpha)
        p_out.append(po)
        s_out.append(so)
    stack = lambda outs, i: jnp.stack([o[i] for o in outs])
    return (hp, hs,
            stack(p_out, 0), stack(p_out, 1), stack(p_out, 2), stack(p_out, 3),
            stack(s_out, 0), stack(s_out, 1), stack(s_out, 2), stack(s_out, 3))
```

```python
import functools

import jax
import jax.numpy as jnp
from jax import lax
from jax.experimental import pallas as pl
from jax.experimental.pallas import tpu as pltpu

F32 = jnp.float32
BF16 = jnp.bfloat16

LANES = 128
SUBLANES = 8
SSD_HEAD_DIM = 64
SSD_GROUPS = 8
SSD_STATE = 128
GDN_HEAD_DIM = 128
CONV_K = 4
NORM_EPS = 1e-6
LN_EPS = 1e-5
CHUNK = 128
NEG_BIG = -1e30
VMEM_LIMIT = 56 * 1024 * 1024


def _divisor_tile(n, target, mult):
    best = None
    t = mult
    while t <= min(n, target):
        if n % t == 0:
            best = t
        t += mult
    assert best is not None, (n, target, mult)
    return best


def _dot(a, b):
    return jnp.dot(a, b, preferred_element_type=F32)


def _dot_nt(a, b):
    return lax.dot_general(a, b, (((1,), (1,)), ((), ())), preferred_element_type=F32)


def _split3(x):
    x1 = x.astype(BF16)
    r1 = x - x1.astype(F32)
    x2 = r1.astype(BF16)
    r2 = r1 - x2.astype(F32)
    return x1, x2, r2.astype(BF16)


def _dot_sel_lhs(sel_bf, x):
    x1, x2, x3 = _split3(x)
    return _dot(sel_bf, x1) + _dot(sel_bf, x2) + _dot(sel_bf, x3)


def _dot_sel_rhs(x, sel_bf):
    x1, x2, x3 = _split3(x)
    return _dot(x1, sel_bf) + _dot(x2, sel_bf) + _dot(x3, sel_bf)


def _sigmoid(x):
    return 1.0 / (1.0 + jnp.exp(-x))


def _silu(x):
    return x * _sigmoid(x)


def _softplus(x):
    return jnp.maximum(x, 0.0) + jnp.log1p(jnp.exp(-jnp.abs(x)))


def _tril01(t, dtype=BF16):
    r = lax.broadcasted_iota(jnp.int32, (t, t), 0)
    c = lax.broadcasted_iota(jnp.int32, (t, t), 1)
    return (r >= c).astype(dtype)


def _expander(rows, cols, per, row_offset=0):
    assert per & (per - 1) == 0
    r = lax.broadcasted_iota(jnp.int32, (rows, cols), 0)
    c = lax.broadcasted_iota(jnp.int32, (rows, cols), 1)
    return (lax.shift_right_logical(c, per.bit_length() - 1) + row_offset == r).astype(BF16)


def _causal_conv_silu(raw_ref, tail_ref, w_ref, bias, t):
    assert CONV_K == 4
    raw = raw_ref[...]
    ext = jnp.concatenate([tail_ref[...], raw], axis=0)
    ext1 = pltpu.roll(ext, 1, 0)
    u = ext1 * w_ref[0:1, :] + ext * w_ref[1:2, :]
    acc = pltpu.roll(u, 2, 0) + (ext1 * w_ref[2:3, :] + ext * w_ref[3:4, :])
    if bias is not None:
        acc = acc + bias
    tail_ref[...] = raw[t - SUBLANES:t, :]
    return _silu(acc[SUBLANES:SUBLANES + t, :])


def _lane_roll(x, shift):
    amount = jnp.where(shift == 0, 0, LANES - shift)
    return pltpu.roll(x, amount, 1)


def _wprep_kernel(a_ref, b0_ref, b1_ref, m0_ref, m1_ref, o_ref, om_ref, *, na, shift, n_misc):
    j = pl.program_id(1)
    tn = o_ref.shape[1]

    @pl.when(j < na)
    def _():
        o_ref[...] = a_ref[...].astype(o_ref.dtype)

    @pl.when(j >= na)
    def _():
        cat = jnp.concatenate([b0_ref[...], b1_ref[...]], axis=1)
        if shift:
            cat = pltpu.roll(cat, cat.shape[1] - shift, 1)
        o_ref[...] = cat[:, 0:tn].astype(o_ref.dtype)

    @pl.when(j == 0)
    def _():
        lane = lax.broadcasted_iota(jnp.int32, m0_ref.shape, 1)
        misc = jnp.where(lane < shift, m0_ref[...], jnp.where(lane < n_misc, m1_ref[...], 0.0))
        om_ref[...] = misc.astype(om_ref.dtype)


def _wprep(w, *, n_a, b_start, n_b):
    k, n_total = w.shape
    shift = b_start % LANES
    b_al = b_start - shift
    b_end = b_start + n_b
    n_misc = shift + (n_total - b_end)
    assert n_a % LANES == 0 and b_start - n_a == shift and b_end % LANES == shift and n_misc <= LANES
    tk = _divisor_tile(k, 1024, 16)
    tn = 1024
    assert n_a % tn == 0 and n_b % tn == 0 and b_al % tn == 0
    na, nb = n_a // tn, n_b // tn
    per = tn // LANES
    return pl.pallas_call(
        functools.partial(_wprep_kernel, na=na, shift=shift, n_misc=n_misc),
        grid=(k // tk, na + nb),
        in_specs=[
            pl.BlockSpec((tk, tn), lambda i, j: (i, jnp.minimum(j, na - 1))),
            pl.BlockSpec((tk, tn), lambda i, j: (i, b_al // tn + jnp.maximum(j - na, 0))),
            pl.BlockSpec((tk, LANES), lambda i, j: (i, (b_al // tn + jnp.maximum(j - na, 0) + 1) * per)),
            pl.BlockSpec((tk, LANES), lambda i, j: (i, n_a // LANES)),
            pl.BlockSpec((tk, LANES), lambda i, j: (i, (b_end - shift) // LANES)),
        ],
        out_specs=[
            pl.BlockSpec((tk, tn), lambda i, j: (i, j)),
            pl.BlockSpec((tk, LANES), lambda i, j: (i, 0)),
        ],
        out_shape=[jax.ShapeDtypeStruct((k, n_a + n_b), BF16), jax.ShapeDtypeStruct((k, LANES), BF16)],
        compiler_params=pltpu.CompilerParams(
            dimension_semantics=("parallel", "arbitrary"), vmem_limit_bytes=VMEM_LIMIT),
        name="wprep",
    )(w, w, w, w, w)


def _inproj_kernel(xp_ref, xs_ref, w_ref, wm_ref, o_ref, om_ref, *, npt, ms):
    i = pl.program_id(0)
    j = pl.program_id(1)

    def emit(x_ref, rows):
        o_ref[0:rows, :] = _dot(x_ref[...], w_ref[...])

        @pl.when(j == 0)
        def _():
            om_ref[0:rows, :] = _dot(x_ref[...], wm_ref[...])

    @pl.when(i < npt)
    def _():
        emit(xp_ref, xp_ref.shape[0])

    @pl.when(i == npt)
    def _():
        emit(xs_ref, ms)


def _inproj(xp_bf, xs_bf, w_main, w_misc):
    mp, k = xp_bf.shape
    ms = xs_bf.shape[0]
    n = w_main.shape[1]
    nm = w_misc.shape[1]
    tm = _divisor_tile(mp, 1024, 16)
    assert ms <= tm
    tn = _divisor_tile(n, 1024, LANES)
    npt = mp // tm
    m = mp + ms
    return pl.pallas_call(
        functools.partial(_inproj_kernel, npt=npt, ms=ms),
        grid=(npt + 1, n // tn),
        in_specs=[
            pl.BlockSpec((tm, k), lambda i, j: (jnp.minimum(i, npt - 1), 0)),
            pl.BlockSpec((ms, k), lambda i, j: (0, 0)),
            pl.BlockSpec((k, tn), lambda i, j: (0, j)),
            pl.BlockSpec((k, nm), lambda i, j: (0, 0)),
        ],
        out_specs=[
            pl.BlockSpec((tm, tn), lambda i, j: (i, j)),
            pl.BlockSpec((tm, nm), lambda i, j: (i, 0)),
        ],
        out_shape=[jax.ShapeDtypeStruct((m, n), F32), jax.ShapeDtypeStruct((m, nm), F32)],
        compiler_params=pltpu.CompilerParams(
            dimension_semantics=("parallel", "arbitrary"), vmem_limit_bytes=VMEM_LIMIT),
        name="inproj",
    )(xp_bf, xs_bf, w_main, w_misc)


SSD_GROUPS_PER_STEP = 8


def _ssd_kernel(z_ref, xs_ref, b_ref, c_ref, misc_ref,
                wx_ref, wb_ref, wc_ref, bx_ref, bb_ref, bc_ref,
                dtb_ref, alog_ref, dvec_ref, nw_ref,
                y_ref, hout_ref,
                ht_ref, tailx_ref, tailb_ref, tailc_ref, *, t, hpg, ngs, nc):
    gblk = pl.program_id(1)
    c = pl.program_id(2)
    wg = hpg * SSD_HEAD_DIM
    n = SSD_STATE
    groups = range(ngs)
    gsl = [slice(g * wg, (g + 1) * wg) for g in groups]
    nsl = [slice(g * n, (g + 1) * n) for g in groups]

    @pl.when(c == 0)
    def _():
        ht_ref[...] = jnp.zeros_like(ht_ref)
        tailx_ref[...] = jnp.zeros_like(tailx_ref)
        tailb_ref[...] = jnp.zeros_like(tailb_ref)
        tailc_ref[...] = jnp.zeros_like(tailc_ref)

    xs = _causal_conv_silu(xs_ref, tailx_ref, wx_ref, bx_ref[...], t)
    bm = _causal_conv_silu(b_ref, tailb_ref, wb_ref, bb_ref[...], t)
    cm = _causal_conv_silu(c_ref, tailc_ref, wc_ref, bc_ref[...], t)

    misc = misc_ref[...]
    dt_all = _softplus(misc + dtb_ref[...])
    da_all = dt_all * (-jnp.exp(alog_ref[...]))
    dt_r = _lane_roll(dt_all, gblk * (ngs * hpg))
    da_r = _lane_roll(da_all, gblk * (ngs * hpg))

    cum = _dot_sel_lhs(_tril01(t), da_r)
    cum_t = cum.T
    expand = _expander(LANES, ngs * wg, SSD_HEAD_DIM)
    cum_b = _dot_sel_rhs(cum, expand)
    dt_b = _dot_sel_rhs(dt_r, expand)
    last_b = cum_b[t - 1:t, :]
    xdt = xs * dt_b
    xw_bf = (xdt * jnp.exp(last_b - cum_b)).astype(BF16)

    bm_bf = [bm[:, nsl[g]].astype(BF16) for g in groups]
    cm_bf = [cm[:, nsl[g]].astype(BF16) for g in groups]
    cb = [_dot_nt(cm_bf[g], bm_bf[g]) for g in groups]
    ht = [ht_ref[g] for g in groups]
    y_inter = [_dot(cm_bf[g], ht[g].astype(BF16)) for g in groups]
    upd = [_dot(bm[:, nsl[g]].T.astype(BF16), xw_bf[:, gsl[g]]) for g in groups]

    rows = lax.broadcasted_iota(jnp.int32, (t, t), 0)
    cols = lax.broadcasted_iota(jnp.int32, (t, t), 1)
    causal = rows >= cols
    lane = lax.broadcasted_iota(jnp.int32, (t, LANES), 1)
    y_intra = []
    for g in groups:
        for p in range(hpg // 2):
            wms = []
            for h in (g * hpg + 2 * p, g * hpg + 2 * p + 1):
                col = jnp.broadcast_to(cum[:, h:h + 1], (t, t))
                row = jnp.broadcast_to(cum_t[h:h + 1, :], (t, t))
                dec = jnp.exp(jnp.where(causal, col - row, NEG_BIG))
                wms.append((cb[g] * dec).astype(BF16))
            xp = xdt[:, g * wg + p * LANES:g * wg + (p + 1) * LANES]
            rhs = jnp.concatenate([jnp.where(lane < SSD_HEAD_DIM, xp, 0.0),
                                   jnp.where(lane >= SSD_HEAD_DIM, xp, 0.0)], axis=0).astype(BF16)
            y_intra.append(_dot(jnp.concatenate(wms, axis=1), rhs))
    y = jnp.concatenate(y_intra, axis=1) if len(y_intra) > 1 else y_intra[0]
    y_int = jnp.concatenate(y_inter, axis=1) if ngs > 1 else y_inter[0]
    y = y + y_int * jnp.exp(cum_b) + xs * dvec_ref[...]
    u = y * _silu(z_ref[...])
    outs = []
    for g in groups:
        ug = u[:, gsl[g]]
        ms = jnp.mean(ug * ug, axis=-1, keepdims=True)
        outs.append(ug * lax.rsqrt(ms + NORM_EPS))
    un = jnp.concatenate(outs, axis=1) if ngs > 1 else outs[0]
    y_ref[...] = (un * nw_ref[...]).astype(y_ref.dtype)

    elast = jnp.exp(last_b)
    for g in groups:
        ht_ref[g] = ht[g] * elast[:, gsl[g]] + upd[g]

    @pl.when(c == nc - 1)
    def _():
        for g in groups:
            hout_ref[0, g * hpg:(g + 1) * hpg] = ht_ref[g].T.reshape(hpg, SSD_HEAD_DIM, SSD_STATE)


def _ssd_prompt(proj, misc, conv_w, conv_b, dtb_pad, alog_pad, dvec, norm_w, *, bsz, seqlen, dz):
    t = CHUNK
    nc = seqlen // t
    hs = dz // SSD_HEAD_DIM
    hpg = hs // SSD_GROUPS
    wg = hpg * SSD_HEAD_DIM
    n = SSD_STATE
    ngs = SSD_GROUPS_PER_STEP
    assert SSD_GROUPS % ngs == 0 and hpg % 2 == 0
    ngb = SSD_GROUPS // ngs
    wv, wn = ngs * wg, ngs * n
    xs_blk = dz // wv
    b_blk = 2 * dz // wn
    c_blk = b_blk + ngb
    cwb = dz // wn
    row = lambda b, g, c: b * nc + c
    kern = functools.partial(_ssd_kernel, t=t, hpg=hpg, ngs=ngs, nc=nc)
    return pl.pallas_call(
        kern,
        grid=(bsz, ngb, nc),
        in_specs=[
            pl.BlockSpec((t, wv), lambda b, g, c: (row(b, g, c), g)),
            pl.BlockSpec((t, wv), lambda b, g, c: (row(b, g, c), xs_blk + g)),
            pl.BlockSpec((t, wn), lambda b, g, c: (row(b, g, c), b_blk + g)),
            pl.BlockSpec((t, wn), lambda b, g, c: (row(b, g, c), c_blk + g)),
            pl.BlockSpec((t, LANES), lambda b, g, c: (row(b, g, c), 0)),
            pl.BlockSpec((CONV_K, wv), lambda b, g, c: (0, g)),
            pl.BlockSpec((CONV_K, wn), lambda b, g, c: (0, cwb + g)),
            pl.BlockSpec((CONV_K, wn), lambda b, g, c: (0, cwb + ngb + g)),
            pl.BlockSpec((1, wv), lambda b, g, c: (0, g)),
            pl.BlockSpec((1, wn), lambda b, g, c: (0, cwb + g)),
            pl.BlockSpec((1, wn), lambda b, g, c: (0, cwb + ngb + g)),
            pl.BlockSpec((1, LANES), lambda b, g, c: (0, 0)),
            pl.BlockSpec((1, LANES), lambda b, g, c: (0, 0)),
            pl.BlockSpec((1, wv), lambda b, g, c: (0, g)),
            pl.BlockSpec((1, wv), lambda b, g, c: (0, g)),
        ],
        out_specs=[
            pl.BlockSpec((t, wv), lambda b, g, c: (row(b, g, c), g)),
            pl.BlockSpec((1, ngs * hpg, SSD_HEAD_DIM, SSD_STATE), lambda b, g, c: (b, g, 0, 0)),
        ],
        out_shape=[
            jax.ShapeDtypeStruct((bsz * seqlen, dz), BF16),
            jax.ShapeDtypeStruct((bsz, hs, SSD_HEAD_DIM, SSD_STATE), F32),
        ],
        scratch_shapes=[
            pltpu.VMEM((ngs, n, wg), F32),
            pltpu.VMEM((SUBLANES, wv), F32),
            pltpu.VMEM((SUBLANES, wn), F32),
            pltpu.VMEM((SUBLANES, wn), F32),
        ],
        compiler_params=pltpu.CompilerParams(
            dimension_semantics=("parallel", "parallel", "arbitrary"), vmem_limit_bytes=VMEM_LIMIT),
        name="ssd_prompt",
    )(proj, proj, proj, proj, misc, conv_w, conv_w, conv_w, conv_b, conv_b, conv_b,
      dtb_pad, alog_pad, dvec, norm_w)


INV_BASE = 16
GDN_HEADS_PER_STEP = 16


def _inverse_masks(t):
    rows = lax.broadcasted_iota(jnp.int32, (t, t), 0)
    cols = lax.broadcasted_iota(jnp.int32, (t, t), 1)
    eye = jnp.where(rows == cols, 1.0, 0.0)
    shift = INV_BASE.bit_length() - 1
    same = lax.shift_right_logical(rows, shift) == lax.shift_right_logical(cols, shift)
    masks = [same]
    b = INV_BASE
    while b < t:
        shift += 1
        same2 = lax.shift_right_logical(rows, shift) == lax.shift_right_logical(cols, shift)
        masks.append(jnp.logical_and(same2, jnp.logical_not(same)))
        same = same2
        b *= 2
    return eye, masks


def _unit_lower_inverse(ns, eye, masks):
    heads = range(len(ns))
    nks = [jnp.where(masks[0], n, 0.0) for n in ns]
    qs = [-nk for nk in nks]
    k = 2
    while k < INV_BASE:
        nk_bfs = [nk.astype(BF16) for nk in nks]
        nks = [_dot(nk_bf, nk_bf) for nk_bf in nk_bfs]
        upd = [_dot(qs[i].astype(BF16), nks[i].astype(BF16)) for i in heads]
        qs = [qs[i] + nks[i] + upd[i] for i in heads]
        k *= 2
    ds = [eye + q for q in qs]
    for off in masks[1:]:
        d_bfs = [d.astype(BF16) for d in ds]
        dc = [_dot(d_bfs[i], jnp.where(off, ns[i], 0.0).astype(BF16)) for i in heads]
        dcd = [_dot(dc[i].astype(BF16), d_bfs[i]) for i in heads]
        ds = [ds[i] - dcd[i] for i in heads]
    return ds


def _gdn_kernel(q_ref, k_ref, v_ref, zg_ref, misc_ref,
                wq_ref, wk_ref, wv_ref, dtb_ref, alog_ref, nw_ref,
                y_ref, sout_ref,
                s_ref, tailq_ref, tailk_ref, tailv_ref, *, t, hb, nc, b_lane, a_lane):
    hblk = pl.program_id(1)
    c = pl.program_id(2)
    d = GDN_HEAD_DIM

    @pl.when(c == 0)
    def _():
        s_ref[...] = jnp.zeros_like(s_ref)
        for tail in (tailq_ref, tailk_ref, tailv_ref):
            tail[...] = jnp.zeros_like(tail)

    q = _causal_conv_silu(q_ref, tailq_ref, wq_ref, None, t)
    k = _causal_conv_silu(k_ref, tailk_ref, wk_ref, None, t)
    v = _causal_conv_silu(v_ref, tailv_ref, wv_ref, None, t)
    zg = zg_ref[...]

    misc = misc_ref[...]
    beta_r = _lane_roll(_sigmoid(misc), b_lane + hblk * hb)
    g_all = -jnp.exp(alog_ref[...]) * _softplus(misc + dtb_ref[...])
    g_r = _lane_roll(g_all, a_lane + hblk * hb)

    tril = _tril01(t)
    cum = _dot_sel_lhs(tril, g_r)
    cum_t = cum.T
    ecum = jnp.exp(cum)
    last = cum[t - 1:t, :]
    wdec = jnp.exp(last - cum)
    elast = jnp.exp(last)

    rows = lax.broadcasted_iota(jnp.int32, (t, t), 0)
    cols = lax.broadcasted_iota(jnp.int32, (t, t), 1)
    causal = rows >= cols
    strict = rows > cols
    eye, inv_masks = _inverse_masks(t)

    heads = range(hb)
    sls = [slice(i * d, (i + 1) * d) for i in heads]
    colb = lambda a, i, w: jnp.broadcast_to(a[:, i:i + 1], (t, w))
    qn = [q[:, sl] * (lax.rsqrt(jnp.sum(q[:, sl] * q[:, sl], axis=-1, keepdims=True) + NORM_EPS) * (d ** -0.5))
          for sl in sls]
    kn = [k[:, sl] * lax.rsqrt(jnp.sum(k[:, sl] * k[:, sl], axis=-1, keepdims=True) + NORM_EPS) for sl in sls]
    dec = [jnp.exp(jnp.where(causal, colb(cum, i, t) - jnp.broadcast_to(cum_t[i:i + 1, :], (t, t)), NEG_BIG))
           for i in heads]
    kn_bf = [x.astype(BF16) for x in kn]
    qn_bf = [x.astype(BF16) for x in qn]
    kk = [_dot_nt(kn_bf[i], kn_bf[i]) for i in heads]
    qk = [_dot_nt(qn_bf[i], kn_bf[i]) for i in heads]
    ns = [jnp.where(strict, kk[i] * dec[i] * colb(beta_r, i, t), 0.0) for i in heads]
    minv = _unit_lower_inverse(ns, eye, inv_masks)
    beta_ecum = beta_r * ecum
    rhs = [jnp.concatenate([v[:, sls[i]] * colb(beta_r, i, d), kn[i] * colb(beta_ecum, i, d)],
                           axis=1).astype(BF16) for i in heads]
    x = [_dot(minv[i].astype(BF16), rhs[i]) for i in heads]
    s_old = [s_ref[i] for i in heads]
    r = [_dot(jnp.concatenate([x[i][:, d:], qn[i] * colb(ecum, i, d)], axis=0).astype(BF16),
              s_old[i].astype(BF16)) for i in heads]
    v_new_bf = [(x[i][:, :d] - r[i][:t]).astype(BF16) for i in heads]
    o = [_dot((qk[i] * dec[i]).astype(BF16), v_new_bf[i]) for i in heads]
    upd = [_dot((kn[i] * colb(wdec, i, d)).T.astype(BF16), v_new_bf[i]) for i in heads]
    outs = []
    for i in heads:
        s_ref[i] = s_old[i] * elast[:, i:i + 1] + upd[i]
        oi = r[i][t:] + o[i]
        ms = jnp.mean(oi * oi, axis=-1, keepdims=True)
        outs.append(oi * lax.rsqrt(ms + NORM_EPS) * nw_ref[...] * _silu(zg[:, sls[i]]))
    y = jnp.concatenate(outs, axis=1) if hb > 1 else outs[0]
    y_ref[...] = y.astype(y_ref.dtype)

    @pl.when(c == nc - 1)
    def _():
        sout_ref[0] = s_ref[...]


def _gdn_prompt(proj, misc, conv_w, dtb_pad, alog_pad, norm_w, *, bsz, seqlen, q_off, zg_off, width,
                b_lane, a_lane):
    t = CHUNK
    nc = seqlen // t
    d = GDN_HEAD_DIM
    hg = width // d
    hb = _divisor_tile(hg, GDN_HEADS_PER_STEP, 1)
    bw = hb * d
    nhb = hg // hb
    qb, kb, vb, zb = q_off // bw, (q_off + width) // bw, (q_off + 2 * width) // bw, zg_off // bw
    wkb, wvb = width // bw, 2 * width // bw
    row = lambda b, h, c: b * nc + c
    kern = functools.partial(_gdn_kernel, t=t, hb=hb, nc=nc, b_lane=b_lane, a_lane=a_lane)
    return pl.pallas_call(
        kern,
        grid=(bsz, nhb, nc),
        in_specs=[
            pl.BlockSpec((t, bw), lambda b, h, c: (row(b, h, c), qb + h)),
            pl.BlockSpec((t, bw), lambda b, h, c: (row(b, h, c), kb + h)),
            pl.BlockSpec((t, bw), lambda b, h, c: (row(b, h, c), vb + h)),
            pl.BlockSpec((t, bw), lambda b, h, c: (row(b, h, c), zb + h)),
            pl.BlockSpec((t, LANES), lambda b, h, c: (row(b, h, c), 0)),
            pl.BlockSpec((CONV_K, bw), lambda b, h, c: (0, h)),
            pl.BlockSpec((CONV_K, bw), lambda b, h, c: (0, wkb + h)),
            pl.BlockSpec((CONV_K, bw), lambda b, h, c: (0, wvb + h)),
            pl.BlockSpec((1, LANES), lambda b, h, c: (0, 0)),
            pl.BlockSpec((1, LANES), lambda b, h, c: (0, 0)),
            pl.BlockSpec((1, d), lambda b, h, c: (0, 0)),
        ],
        out_specs=[
            pl.BlockSpec((t, bw), lambda b, h, c: (row(b, h, c), h)),
            pl.BlockSpec((1, hb, d, d), lambda b, h, c: (b, h, 0, 0)),
        ],
        out_shape=[
            jax.ShapeDtypeStruct((bsz * seqlen, width), BF16),
            jax.ShapeDtypeStruct((bsz, hg, d, d), F32),
        ],
        scratch_shapes=[
            pltpu.VMEM((hb, d, d), F32),
            pltpu.VMEM((SUBLANES, bw), F32),
            pltpu.VMEM((SUBLANES, bw), F32),
            pltpu.VMEM((SUBLANES, bw), F32),
        ],
        compiler_params=pltpu.CompilerParams(
            dimension_semantics=("parallel", "parallel", "arbitrary"), vmem_limit_bytes=VMEM_LIMIT),
        name="gdn_prompt",
    )(proj, proj, proj, proj, misc, conv_w, conv_w, conv_w, dtb_pad, alog_pad, norm_w)


def _sample_conv_kernel(s0_ref, s1_ref, s2_ref, u_ref, w_ref, b_ref, o_ref):
    acc = b_ref[...] + s0_ref[...] * w_ref[0:1, :]
    acc = acc + s1_ref[...] * w_ref[1:2, :]
    acc = acc + s2_ref[...] * w_ref[2:3, :]
    acc = acc + u_ref[...] * w_ref[3:4, :]
    o_ref[...] = _silu(acc)


def _sample_conv(state2d, proj, conv_w, conv_b, *, ms, row_blk, col_off, width):
    wb = _divisor_tile(width, 512, LANES)
    nb = width // wb
    cb = col_off // wb
    return pl.pallas_call(
        _sample_conv_kernel,
        grid=(nb,),
        in_specs=[
            pl.BlockSpec((ms, wb), lambda j: (0, j)),
            pl.BlockSpec((ms, wb), lambda j: (0, nb + j)),
            pl.BlockSpec((ms, wb), lambda j: (0, 2 * nb + j)),
            pl.BlockSpec((ms, wb), lambda j: (row_blk, cb + j)),
            pl.BlockSpec((CONV_K, wb), lambda j: (0, j)),
            pl.BlockSpec((1, wb), lambda j: (0, j)),
        ],
        out_specs=pl.BlockSpec((ms, wb), lambda j: (0, j)),
        out_shape=jax.ShapeDtypeStruct((ms, width), F32),
        compiler_params=pltpu.CompilerParams(dimension_semantics=("parallel",)),
        name="sample_conv",
    )(state2d, state2d, state2d, proj, conv_w, conv_b)


def _sample_scalars_kernel(misc_ref, dtb_ref, alog_s_ref, dtb_g_ref, alog_g_ref, dt_ref, sc_ref, *, hs, hg):
    misc = misc_ref[...]
    lane = lax.broadcasted_iota(jnp.int32, misc.shape, 1)
    dt = _softplus(misc + dtb_ref[...])
    eda = jnp.exp(dt * (-jnp.exp(alog_s_ref[...])))
    beta = _sigmoid(misc)
    eg = jnp.exp(-jnp.exp(alog_g_ref[...]) * _softplus(misc + dtb_g_ref[...]))
    dt_ref[...] = dt
    sc_ref[...] = jnp.where(lane < hs, eda, jnp.where(lane < hs + hg, beta, eg))


def _sample_scalars(misc, dtb_s, alog_s, dtb_g, alog_g, *, ms, row_blk, hs, hg):
    vec = pl.BlockSpec((1, LANES), lambda i: (0, 0))
    return pl.pallas_call(
        functools.partial(_sample_scalars_kernel, hs=hs, hg=hg),
        grid=(1,),
        in_specs=[pl.BlockSpec((ms, LANES), lambda i: (row_blk, 0)), vec, vec, vec, vec],
        out_specs=[pl.BlockSpec((ms, LANES), lambda i: (0, 0)), pl.BlockSpec((ms, LANES), lambda i: (0, 0))],
        out_shape=[jax.ShapeDtypeStruct((ms, LANES), F32), jax.ShapeDtypeStruct((ms, LANES), F32)],
        name="sample_scalars",
    )(misc, dtb_s, alog_s, dtb_g, alog_g)


def _transpose_blocks(x, nblk):
    return jnp.concatenate([x[:, j * LANES:(j + 1) * LANES].T for j in range(nblk)], axis=0)


def _row_group(b):
    b8 = pl.multiple_of(lax.shift_right_logical(b, 3) * SUBLANES, SUBLANES)
    in_group = lax.broadcasted_iota(jnp.int32, (SUBLANES, LANES), 0) == jnp.bitwise_and(b, SUBLANES - 1)
    return b8, in_group


def _untranspose_blocks(x, nblk):
    return jnp.concatenate([x[j * LANES:(j + 1) * LANES, :].T for j in range(nblk)], axis=1)


def _ssd_sample_kernel(sc_smem, h_ref, xbc_ref, dt_ref, z_ref, dvec_ref, nw_ref,
                       y_ref, hout_ref,
                       xt_ref, yt_ref, *, ms, hs, dz):
    b = pl.program_id(0)
    hpg = hs // SSD_GROUPS
    wg = hpg * SSD_HEAD_DIM

    @pl.when(b == 0)
    def _():
        xs = xbc_ref[:, 0:dz]
        dt_b = _dot_sel_rhs(dt_ref[...], _expander(LANES, dz, SSD_HEAD_DIM))
        xt_ref[...] = _transpose_blocks(xs * dt_b, dz // LANES)
        yt_ref[...] = jnp.zeros_like(yt_ref)

    seq_rows = lax.broadcasted_iota(jnp.int32, (ms, SSD_STATE), 0)
    seq_lanes = lax.broadcasted_iota(jnp.int32, (wg, LANES), 1)
    b8, in_group = _row_group(b)
    for g in range(SSD_GROUPS):
        hgrp = h_ref[0, g * hpg:(g + 1) * hpg].reshape(wg, SSD_STATE)
        bm = xbc_ref[:, dz + g * SSD_STATE: dz + (g + 1) * SSD_STATE]
        bm = jnp.where(seq_rows == b, bm, 0.0).astype(BF16)
        upd = _dot(xt_ref[g * wg:(g + 1) * wg, :].astype(BF16), bm)
        decay = jnp.concatenate(
            [jnp.full((SSD_HEAD_DIM, SSD_STATE), sc_smem[b, g * hpg + h], F32) for h in range(hpg)], axis=0)
        hn = hgrp * decay + upd
        hout_ref[0, g * hpg:(g + 1) * hpg] = hn.reshape(hpg, SSD_HEAD_DIM, SSD_STATE)
        c_off = dz + SSD_GROUPS * SSD_STATE + g * SSD_STATE
        crow = jnp.sum(jnp.where(in_group, xbc_ref[pl.ds(b8, SUBLANES), c_off:c_off + SSD_STATE], 0.0),
                       axis=0, keepdims=True)
        c_rows = jnp.broadcast_to(crow, (LANES, SSD_STATE)).astype(BF16)
        y_lanes = _dot_nt(hn.astype(BF16), c_rows)
        cur = yt_ref[g * wg:(g + 1) * wg, :]
        yt_ref[g * wg:(g + 1) * wg, :] = jnp.where(seq_lanes == b, y_lanes, cur)

    @pl.when(b == ms - 1)
    def _():
        y = _untranspose_blocks(yt_ref[...], dz // LANES)
        y = y + xbc_ref[:, 0:dz] * dvec_ref[...]
        u = y * _silu(z_ref[...])
        outs = []
        for g in range(SSD_GROUPS):
            ug = u[:, g * wg:(g + 1) * wg]
            msq = jnp.mean(ug * ug, axis=-1, keepdims=True)
            outs.append(ug * lax.rsqrt(msq + NORM_EPS))
        y_ref[...] = (jnp.concatenate(outs, axis=1) * nw_ref[...]).astype(y_ref.dtype)


def _ssd_sample(scal, h_state, xbc_s, dt_s, proj, dvec, norm_w, *, ms, row_blk, dz):
    hs = dz // SSD_HEAD_DIM
    cw = xbc_s.shape[1]
    full = lambda shape: pl.BlockSpec(shape, lambda b: tuple(0 for _ in shape))
    return pl.pallas_call(
        functools.partial(_ssd_sample_kernel, ms=ms, hs=hs, dz=dz),
        grid=(ms,),
        in_specs=[
            pl.BlockSpec(memory_space=pltpu.SMEM),
            pl.BlockSpec((1, hs, SSD_HEAD_DIM, SSD_STATE), lambda b: (b, 0, 0, 0)),
            full((ms, cw)),
            full((ms, LANES)),
            pl.BlockSpec((ms, dz), lambda b: (row_blk, 0)),
            full((1, dz)),
            full((1, dz)),
        ],
        out_specs=[
            full((ms, dz)),
            pl.BlockSpec((1, hs, SSD_HEAD_DIM, SSD_STATE), lambda b: (b, 0, 0, 0)),
        ],
        out_shape=[
            jax.ShapeDtypeStruct((ms, dz), BF16),
            jax.ShapeDtypeStruct(h_state.shape, F32),
        ],
        scratch_shapes=[pltpu.VMEM((dz, LANES), F32), pltpu.VMEM((dz, LANES), F32)],
        compiler_params=pltpu.CompilerParams(
            dimension_semantics=("arbitrary",), vmem_limit_bytes=VMEM_LIMIT),
        name="ssd_sample",
    )(scal, h_state, xbc_s, dt_s, proj, dvec, norm_w)


def _gdn_sample_kernel(sc_smem, s_ref, qkv_ref, sc_ref, zg0_ref, zg1_ref, nw_ref,
                       y_ref, sout_ref,
                       w_ref, qe_ref, vb_ref, qk_ref, kt_ref, o_ref, *, ms, hs, hg):
    b = pl.program_id(0)
    d = GDN_HEAD_DIM
    width = hg * d

    @pl.when(b == 0)
    def _():
        sc = sc_ref[...]
        beta_b = _dot_sel_rhs(sc, _expander(LANES, width, d, hs))
        eg_b = _dot_sel_rhs(sc, _expander(LANES, width, d, hs + hg))
        for h in range(hg):
            sl = slice(h * d, (h + 1) * d)
            qh = qkv_ref[:, h * d:(h + 1) * d]
            kh = qkv_ref[:, width + h * d: width + (h + 1) * d]
            vh = qkv_ref[:, 2 * width + h * d: 2 * width + (h + 1) * d]
            qn = qh * lax.rsqrt(jnp.sum(qh * qh, axis=-1, keepdims=True) + NORM_EPS) * (d ** -0.5)
            kn = kh * lax.rsqrt(jnp.sum(kh * kh, axis=-1, keepdims=True) + NORM_EPS)
            w_ref[:, sl] = kn * beta_b[:, sl] * eg_b[:, sl]
            qe_ref[:, sl] = qn * eg_b[:, sl]
            vb_ref[:, sl] = vh * beta_b[:, sl]
            qk_ref[:, sl] = jnp.broadcast_to(jnp.sum(qn * kn, axis=-1, keepdims=True), (ms, d))
            kt_ref[sl, :] = kn.T

    seq_rows = lax.broadcasted_iota(jnp.int32, (ms, d), 0)
    b8, in_group = _row_group(b)
    for h in range(hg):
        sl = slice(h * d, (h + 1) * d)
        s = s_ref[0, h]
        w8 = jnp.where(in_group, w_ref[pl.ds(b8, SUBLANES), sl], 0.0)
        q8 = jnp.where(in_group, qe_ref[pl.ds(b8, SUBLANES), sl], 0.0)
        r = _dot(jnp.concatenate([w8, q8], axis=0).astype(BF16), s.astype(BF16))
        vb8 = jnp.where(in_group, vb_ref[pl.ds(b8, SUBLANES), sl], 0.0)
        v_new8 = vb8 - r[0:SUBLANES]
        v_new = jnp.sum(v_new8, axis=0, keepdims=True)
        o8 = r[SUBLANES:] + qk_ref[pl.ds(b8, SUBLANES), sl] * v_new8
        o_ref[pl.ds(b8, SUBLANES), sl] = jnp.where(in_group, o8, o_ref[pl.ds(b8, SUBLANES), sl])
        vm = jnp.where(seq_rows == b, jnp.broadcast_to(v_new, (ms, d)), 0.0).astype(BF16)
        s_new = s * sc_smem[b, hs + hg + h] + _dot(kt_ref[sl, :].astype(BF16), vm)
        sout_ref[0, h] = s_new

    @pl.when(b == ms - 1)
    def _():
        outs = []
        for h in range(hg):
            sl = slice(h * d, (h + 1) * d)
            o = o_ref[:, sl]
            msq = jnp.mean(o * o, axis=-1, keepdims=True)
            zg_ref, zoff = (zg0_ref, 0) if h < hg // 2 else (zg1_ref, width // 2)
            zg = zg_ref[:, h * d - zoff:(h + 1) * d - zoff]
            outs.append(o * lax.rsqrt(msq + NORM_EPS) * nw_ref[...] * _silu(zg))
        y_ref[...] = jnp.concatenate(outs, axis=1).astype(y_ref.dtype)


def _gdn_sample(scal, s_state, qkv_s, proj, norm_w, *, ms, row_blk, hs, zg_off):
    d = GDN_HEAD_DIM
    hg = s_state.shape[1]
    width = hg * d
    full = lambda shape: pl.BlockSpec(shape, lambda b: tuple(0 for _ in shape))
    act = pltpu.VMEM((ms, width), F32)
    return pl.pallas_call(
        functools.partial(_gdn_sample_kernel, ms=ms, hs=hs, hg=hg),
        grid=(ms,),
        in_specs=[
            pl.BlockSpec(memory_space=pltpu.SMEM),
            pl.BlockSpec((1, hg, d, d), lambda b: (b, 0, 0, 0)),
            full((ms, 3 * width)),
            full((ms, LANES)),
            pl.BlockSpec((ms, width // 2), lambda b: (row_blk, zg_off // (width // 2))),
            pl.BlockSpec((ms, width // 2), lambda b: (row_blk, zg_off // (width // 2) + 1)),
            full((1, d)),
        ],
        out_specs=[
            full((ms, width)),
            pl.BlockSpec((1, hg, d, d), lambda b: (b, 0, 0, 0)),
        ],
        out_shape=[
            jax.ShapeDtypeStruct((ms, width), BF16),
            jax.ShapeDtypeStruct(s_state.shape, F32),
        ],
        scratch_shapes=[act, act, act, act, pltpu.VMEM((width, LANES), F32), act],
        compiler_params=pltpu.CompilerParams(
            dimension_semantics=("arbitrary",), vmem_limit_bytes=VMEM_LIMIT),
        name="gdn_sample",
    )(scal, s_state, qkv_s, scal, proj, proj, norm_w)


def _outproj_kernel(ya_ref, yb_ref, w_ref, x_ref, g_ref, b_ref, o_ref, *, nkh, nk, alpha):
    k = pl.program_id(1)

    @pl.when(k == 0)
    def _():
        o_ref[...] = jnp.zeros_like(o_ref)

    @pl.when(k < nkh)
    def _():
        o_ref[...] += _dot(ya_ref[...], w_ref[...])

    @pl.when(k >= nkh)
    def _():
        o_ref[...] += _dot(yb_ref[...], w_ref[...])

    @pl.when(k == nk - 1)
    def _():
        y = alpha * x_ref[...] + o_ref[...]
        mu = jnp.mean(y, axis=-1, keepdims=True)
        yc = y - mu
        var = jnp.mean(yc * yc, axis=-1, keepdims=True)
        o_ref[...] = yc * lax.rsqrt(var + LN_EPS) * g_ref[...] + b_ref[...]


def _outproj(ya, yb, w_bf, x, ln_g, ln_b, *, alpha):
    m, ka = ya.shape
    kb = yb.shape[1]
    d = w_bf.shape[1]
    tm = _divisor_tile(m, 512, SUBLANES)
    tk = _divisor_tile(min(ka, kb), 1024, LANES)
    assert ka % tk == 0 and kb % tk == 0
    nkh = ka // tk
    nk = nkh + kb // tk
    return pl.pallas_call(
        functools.partial(_outproj_kernel, nkh=nkh, nk=nk, alpha=alpha),
        grid=(m // tm, nk),
        in_specs=[
            pl.BlockSpec((tm, tk), lambda i, k: (i, jnp.minimum(k, nkh - 1))),
            pl.BlockSpec((tm, tk), lambda i, k: (i, jnp.maximum(k - nkh, 0))),
            pl.BlockSpec((tk, d), lambda i, k: (k, 0)),
            pl.BlockSpec((tm, d), lambda i, k: (i, 0), pipeline_mode=pl.Buffered(1)),
            pl.BlockSpec((1, d), lambda i, k: (0, 0)),
            pl.BlockSpec((1, d), lambda i, k: (0, 0)),
        ],
        out_specs=pl.BlockSpec((tm, d), lambda i, k: (i, 0)),
        out_shape=jax.ShapeDtypeStruct((m, d), F32),
        compiler_params=pltpu.CompilerParams(
            dimension_semantics=("parallel", "arbitrary"), vmem_limit_bytes=VMEM_LIMIT),
        name="outproj",
    )(ya, yb, w_bf, x, ln_g, ln_b)


def _pad_lanes(vec, offset):
    out = jnp.zeros((1, LANES), F32)
    return lax.dynamic_update_slice(out, vec.astype(F32)[None, :], (0, offset))


def _layer(xp, xs_in, st_ssd, st_ssd_conv, st_gdn, st_gdn_conv, w_in, ssd_conv_w, ssd_conv_b, ssd_dt_bias,
           ssd_a_log, ssd_d, ssd_norm_w, gdn_conv_w, gdn_dt_bias, gdn_a_log, gdn_norm_w, w_out, ln_g, ln_b,
           *, alpha):
    bsz, seqlen, dm = xp.shape
    ms = xs_in.shape[0]
    mp = bsz * seqlen
    dz = ssd_norm_w.shape[0]
    hs = dz // SSD_HEAD_DIM
    dxbc = ssd_conv_w.shape[1]
    dqkv = gdn_conv_w.shape[1]
    gw = dqkv // 3
    hg = gw // GDN_HEAD_DIM
    assert hs + 2 * hg <= LANES and ms == LANES and seqlen % CHUNK == 0 and mp % ms == 0
    o_dt = dz + dxbc
    o_qkv = o_dt + hs
    o_zg = o_qkv + dqkv
    o_b = o_zg + gw

    w_main, w_misc = _wprep(w_in, n_a=o_dt, b_start=o_qkv, n_b=o_b - o_qkv)
    proj, misc = _inproj(xp.reshape(mp, dm).astype(BF16), xs_in.reshape(ms, dm).astype(BF16), w_main, w_misc)
    q_off = dz + dxbc
    zg_off = q_off + dqkv
    row_blk = mp // ms

    dtb_s = _pad_lanes(ssd_dt_bias, 0)
    alog_s = _pad_lanes(ssd_a_log, 0)
    dtb_g = _pad_lanes(gdn_dt_bias, hs + hg)
    alog_g = _pad_lanes(gdn_a_log, hs + hg)
    dvec = jnp.repeat(ssd_d.astype(F32), SSD_HEAD_DIM)[None, :]
    nw_s = ssd_norm_w.astype(F32)[None, :]
    nw_g = gdn_norm_w.astype(F32)[None, :]
    cb_s = ssd_conv_b.astype(F32)[None, :]

    y_ssd, h_p = _ssd_prompt(proj, misc, ssd_conv_w, cb_s, dtb_s, alog_s, dvec, nw_s,
                             bsz=bsz, seqlen=seqlen, dz=dz)
    y_gdn, s_p = _gdn_prompt(proj, misc, gdn_conv_w, dtb_g, alog_g, nw_g, bsz=bsz, seqlen=seqlen,
                             q_off=q_off, zg_off=zg_off, width=gw, b_lane=hs, a_lane=hs + hg)
    w_out_bf = w_out.astype(BF16)
    g2, b2 = ln_g.astype(F32)[None, :], ln_b.astype(F32)[None, :]
    xp_new = _outproj(y_ssd, y_gdn, w_out_bf, xp.reshape(mp, dm), g2, b2, alpha=alpha).reshape(bsz, seqlen, dm)

    xbc_s = _sample_conv(st_ssd_conv.reshape(ms, -1), proj, ssd_conv_w, cb_s,
                         ms=ms, row_blk=row_blk, col_off=dz, width=dxbc)
    qkv_s = _sample_conv(st_gdn_conv.reshape(ms, -1), proj, gdn_conv_w, jnp.zeros((1, dqkv), F32),
                         ms=ms, row_blk=row_blk, col_off=q_off, width=dqkv)
    dt_s, scal = _sample_scalars(misc, dtb_s, alog_s, dtb_g, alog_g, ms=ms, row_blk=row_blk, hs=hs, hg=hg)
    ys_ssd, h_s = _ssd_sample(scal, st_ssd, xbc_s, dt_s, proj, dvec, nw_s, ms=ms, row_blk=row_blk, dz=dz)
    ys_gdn, s_s = _gdn_sample(scal, st_gdn, qkv_s, proj, nw_g, ms=ms, row_blk=row_blk, hs=hs, zg_off=zg_off)
    xs_new = _outproj(ys_ssd, ys_gdn, w_out_bf, xs_in.reshape(ms, dm), g2, b2, alpha=alpha).reshape(ms, 1, dm)

    tail = lambda lo, hi: jnp.stack(
        [proj[(b + 1) * seqlen - (CONV_K - 1):(b + 1) * seqlen, lo:hi] for b in range(bsz)])
    conv_ssd_p = tail(dz, dz + dxbc)
    conv_gdn_p = tail(q_off, q_off + dqkv)
    raw_s = proj[mp:]
    conv_ssd_s = jnp.concatenate([st_ssd_conv[:, 1:], raw_s[:, None, dz:dz + dxbc]], axis=1)
    conv_gdn_s = jnp.concatenate([st_gdn_conv[:, 1:], raw_s[:, None, q_off:q_off + dqkv]], axis=1)
    return (xp_new, xs_new, (h_p, conv_ssd_p, s_p, conv_gdn_p), (h_s, conv_ssd_s, s_s, conv_gdn_s))


def kernel(x_prompt, x_sample, state_ssd, state_ssd_conv, state_gdn, state_gdn_conv, w_in, ssd_conv_w, ssd_conv_b,
           ssd_dt_bias, ssd_a_log, ssd_d, ssd_norm_w, gdn_conv_w, gdn_dt_bias, gdn_a_log, gdn_norm_w, w_out,
           ln_g, ln_b):
    depth = w_in.shape[0]
    alpha = (2 * depth) ** 0.25
    assert x_sample.shape[1] == 1
    hp, hs = x_prompt, x_sample
    p_out, s_out = [], []
    for l in range(depth):
        hp, hs, po, so = _layer(
            hp, hs, state_ssd[l], state_ssd_conv[l], state_gdn[l], state_gdn_conv[l], w_in[l], ssd_conv_w[l],
            ssd_conv_b[l], ssd_dt_bias[l], ssd_a_log[l], ssd_d[l], ssd_norm_w[l], gdn_conv_w[l], gdn_dt_bias[l],
            gdn_a_log[l], gdn_norm_w[l], w_out[l], ln_g[l], ln_b[l], alpha=alpha)
        p_out.append(po)
        s_out.append(so)
    stack = lambda outs, i: jnp.stack([o[i] for o in outs])
    return (hp, hs,
            stack(p_out, 0), stack(p_out, 1), stack(p_out, 2), stack(p_out, 3),
            stack(s_out, 0), stack(s_out, 1), stack(s_out, 2), stack(s_out, 3))
```
